```python
import jax, jax.numpy as jnp
from jax import lax
import numpy as np

D_MODEL = 1024
BATCH = 2
SEQ = 8192
DEPTH = 4

CHUNK = 64
N_META = 16
HEAD_DIM = 64
H_FOX = 8
H_RWKV = 8
W_FOX = H_FOX * HEAD_DIM
W_RWKV = H_RWKV * HEAD_DIM
LORA_W = 64
LORA_A = 64
LORA_G = 128
N_RWKV_IN = 3 * W_RWKV + LORA_W + LORA_A + LORA_G
N_GATES = 2 * D_MODEL
N_IN = 3 * W_FOX + H_FOX + N_RWKV_IN + N_GATES
D_FF = 2816
CONV_W = 3
Q_BLOCK = 128
NORM_EPS = 1e-6
GN_EPS = HEAD_DIM * 1e-5

IN_SPLITS = (W_FOX, 2 * W_FOX, 3 * W_FOX, 3 * W_FOX + H_FOX, 3 * W_FOX + H_FOX + N_RWKV_IN)
RWKV_SPLITS = (W_RWKV, 2 * W_RWKV, 3 * W_RWKV, 3 * W_RWKV + LORA_W, 3 * W_RWKV + LORA_W + LORA_A)

kernel_name = 'hybrid_fox_rwkv7_streaming_block'


def rms_norm(x, g):
    xf = x.astype(jnp.float32)
    y = xf * lax.rsqrt(jnp.mean(xf * xf, axis=-1, keepdims=True) + NORM_EPS)
    return y.astype(x.dtype) * g


def fox_attention(q, k, v, fcum):
    b, h, length, n = q.shape
    nb = length // Q_BLOCK
    q_blocks = jnp.moveaxis(q.reshape(b, h, nb, Q_BLOCK, n), 2, 0)
    f_blocks = jnp.moveaxis(fcum.reshape(b, h, nb, Q_BLOCK), 2, 0)
    starts = jnp.arange(nb, dtype=jnp.int32) * Q_BLOCK
    key_pos = jnp.arange(length, dtype=jnp.int32)
    scale = HEAD_DIM ** -0.5

    def one_block(args):
        q_blk, f_blk, start = args
        s = jnp.einsum('bhqd,bhkd->bhqk', q_blk, k).astype(jnp.float32) * scale
        s = s + f_blk[..., :, None] - fcum[..., None, :]
        q_pos = start + jnp.arange(Q_BLOCK, dtype=jnp.int32)
        mask = key_pos[None, :] <= q_pos[:, None]
        s = jnp.where(mask, s, -jnp.inf)
        p = jax.nn.softmax(s, axis=-1)
        return jnp.einsum('bhqk,bhkd->bhqd', p.astype(v.dtype), v)

    out = lax.map(one_block, (q_blocks, f_blocks, starts))
    return jnp.moveaxis(out, 0, 2).reshape(b, h, length, n)


def rwkv7_scan(r, w, k, v, a, bb):
    b, length, h, n = r.shape

    def step(state, inp):
        r_t, w_t, k_t, v_t, a_t, b_t = inp
        sa = jnp.einsum('bhvk,bhk->bhv', state, a_t)
        state = (state * w_t[:, :, None, :]
                 + sa[..., None] * b_t[:, :, None, :]
                 + v_t[..., None] * k_t[:, :, None, :])
        return state, jnp.einsum('bhvk,bhk->bhv', state, r_t)

    xs = tuple(jnp.moveaxis(t.astype(jnp.float32), 1, 0) for t in (r, w, k, v, a, bb))
    s0 = jnp.zeros((b, h, n, n), jnp.float32)
    _, y = lax.scan(step, s0, xs)
    return jnp.moveaxis(y, 0, 1)


def hybrid_mixer(hn, w_in, q_g, k_g, f_b, mu, w0, w_up, a0, a_up, g_up, k_k, k_a, r_k,
                 gn_w, gn_b, w_bf, w_br, g_b, w_o):
    b, length, _ = hn.shape
    proj = hn @ w_in
    q, k, v, f_logit, rw, gates = jnp.split(proj, IN_SPLITS, axis=-1)

    def to_heads(t, nh):
        return t.reshape(b, length, nh, HEAD_DIM).transpose(0, 2, 1, 3)
    qh = rms_norm(to_heads(q, H_FOX), q_g)
    kh = rms_norm(to_heads(k, H_FOX), k_g)
    vh = to_heads(v, H_FOX)
    log_f = jax.nn.log_sigmoid((f_logit + f_b).astype(jnp.float32))
    fcum = jnp.cumsum(log_f, axis=1).transpose(0, 2, 1)
    o_fox = fox_attention(qh, kh, vh, fcum)
    o_fox = o_fox.transpose(0, 2, 1, 3).reshape(b, length, W_FOX)

    rw_prev = jnp.pad(rw, ((0, 0), (1, 0), (0, 0)))[:, :-1]
    z = rw + mu * (rw_prev - rw)
    r, kr, vr, zw, za, zg = jnp.split(z, RWKV_SPLITS, axis=-1)
    w_log = -jax.nn.softplus(-(w0 + jnp.tanh(zw) @ w_up)) - 0.5
    decay = jnp.exp(-jnp.exp(w_log.astype(jnp.float32)))
    a_rate = jax.nn.sigmoid(a0 + za @ a_up)
    g = jax.nn.sigmoid(zg) @ g_up
    kk = (kr * k_k).reshape(b, length, H_RWKV, HEAD_DIM).astype(jnp.float32)
    kk = kk / jnp.maximum(jnp.sqrt(jnp.sum(kk * kk, axis=-1, keepdims=True)), 1e-12)
    kr = kr * (1.0 + (a_rate - 1.0) * k_a)
    heads4 = lambda t: t.reshape(b, length, H_RWKV, HEAD_DIM)
    r4, k4, v4, a4 = heads4(r), heads4(kr), heads4(vr), heads4(a_rate)
    y = rwkv7_scan(r4, heads4(decay), k4, v4, -kk, kk * a4)
    mean = jnp.mean(y, axis=-1, keepdims=True)
    var = jnp.mean(jnp.square(y - mean), axis=-1, keepdims=True)
    yn = (y - mean) * lax.rsqrt(var + GN_EPS)
    yn = yn * gn_w.reshape(H_RWKV, HEAD_DIM) + gn_b.reshape(H_RWKV, HEAD_DIM)
    bonus = jnp.sum((r4 * k4 * r_k).astype(jnp.float32), axis=-1, keepdims=True) * v4
    o_rwkv = (yn + bonus).astype(hn.dtype).reshape(b, length, W_RWKV) * g

    g_fox, g_rwkv = jnp.split(jax.nn.sigmoid(gates + g_b), 2, axis=-1)
    merged = g_fox * (o_fox @ w_bf) + g_rwkv * (o_rwkv @ w_br)
    return merged @ w_o


def conv_ffn(hn, w_up, w_conv, w_down):
    u = hn @ w_up
    length = u.shape[1]
    u_pad = jnp.pad(u, ((0, 0), (CONV_W - 1, 0), (0, 0)))
    uc = sum(w_conv[j] * u_pad[:, j:j + length] for j in range(CONV_W))
    gate, val = jnp.split(uc, 2, axis=-1)
    return (jax.nn.silu(gate) * val) @ w_down


def setup_inputs(seed: int = 0) -> dict:
    key = jax.random.key(seed)
    ks = jax.random.split(key, 26)
    f32 = jnp.float32

    def nrm(k, shape, s):
        return s * jax.random.normal(k, shape, f32)

    return {
        'x': nrm(ks[0], (BATCH, SEQ, D_MODEL), 1.0),
        'meta_tokens': nrm(ks[1], (N_META, D_MODEL), 1.0),
        'norm_mix': 1.0 + nrm(ks[2], (DEPTH, D_MODEL), 0.02),
        'norm_ffn': 1.0 + nrm(ks[3], (DEPTH, D_MODEL), 0.02),
        'w_in': nrm(ks[4], (DEPTH, D_MODEL, N_IN), D_MODEL ** -0.5),
        'fox_q_norm': 1.0 + nrm(ks[5], (DEPTH, HEAD_DIM), 0.02),
        'fox_k_norm': 1.0 + nrm(ks[6], (DEPTH, HEAD_DIM), 0.02),
        'fox_f_bias': 2.0 + nrm(ks[7], (DEPTH, H_FOX), 1.0),
        'rwkv_shift_mu': jax.random.uniform(ks[8], (DEPTH, N_RWKV_IN), f32),
        'rwkv_w0': -3.0 + nrm(ks[9], (DEPTH, W_RWKV), 1.0),
        'rwkv_w_up': nrm(ks[10], (DEPTH, LORA_W, W_RWKV), 0.1 * LORA_W ** -0.5),
        'rwkv_a0': nrm(ks[11], (DEPTH, W_RWKV), 0.5),
        'rwkv_a_up': nrm(ks[12], (DEPTH, LORA_A, W_RWKV), 0.5 * LORA_A ** -0.5),
        'rwkv_g_up': nrm(ks[13], (DEPTH, LORA_G, W_RWKV), LORA_G ** -0.5),
        'rwkv_k_k': 0.85 + nrm(ks[14], (DEPTH, W_RWKV), 0.1),
        'rwkv_k_a': 1.0 + nrm(ks[15], (DEPTH, W_RWKV), 0.1),
        'rwkv_r_k': nrm(ks[16], (DEPTH, H_RWKV, HEAD_DIM), 0.1),
        'rwkv_gn_w': 1.0 + nrm(ks[17], (DEPTH, W_RWKV), 0.02),
        'rwkv_gn_b': nrm(ks[18], (DEPTH, W_RWKV), 0.02),
        'w_branch_fox': nrm(ks[19], (DEPTH, W_FOX, D_MODEL), W_FOX ** -0.5),
        'w_branch_rwkv': nrm(ks[20], (DEPTH, W_RWKV, D_MODEL), W_RWKV ** -0.5),
        'gate_bias': nrm(ks[21], (DEPTH, N_GATES), 0.1),
        'w_out': nrm(ks[22], (DEPTH, D_MODEL, D_MODEL), 0.5 * D_MODEL ** -0.5),
        'ffn_up': nrm(ks[23], (DEPTH, D_MODEL, 2 * D_FF), D_MODEL ** -0.5),
        'ffn_conv': nrm(ks[24], (DEPTH, CONV_W, 2 * D_FF), CONV_W ** -0.5),
        'ffn_down': nrm(ks[25], (DEPTH, D_FF, D_MODEL), 0.5 * D_FF ** -0.5),
    }


def reference(x, meta_tokens, norm_mix, norm_ffn, w_in, fox_q_norm, fox_k_norm, fox_f_bias,
              rwkv_shift_mu, rwkv_w0, rwkv_w_up, rwkv_a0, rwkv_a_up, rwkv_g_up, rwkv_k_k,
              rwkv_k_a, rwkv_r_k, rwkv_gn_w, rwkv_gn_b, w_branch_fox, w_branch_rwkv,
              gate_bias, w_out, ffn_up, ffn_conv, ffn_down):
    b, s, _ = x.shape
    meta = jnp.broadcast_to(meta_tokens.astype(x.dtype)[None], (b, N_META, D_MODEL))
    h = jnp.concatenate([meta, x], axis=1)
    length = N_META + s
    padded = -(-length // Q_BLOCK) * Q_BLOCK
    h = jnp.pad(h, ((0, 0), (0, padded - length), (0, 0)))
    for i in range(DEPTH):
        h = h + hybrid_mixer(rms_norm(h, norm_mix[i]), w_in[i], fox_q_norm[i], fox_k_norm[i],
                             fox_f_bias[i], rwkv_shift_mu[i], rwkv_w0[i], rwkv_w_up[i],
                             rwkv_a0[i], rwkv_a_up[i], rwkv_g_up[i], rwkv_k_k[i], rwkv_k_a[i],
                             rwkv_r_k[i], rwkv_gn_w[i], rwkv_gn_b[i], w_branch_fox[i],
                             w_branch_rwkv[i], gate_bias[i], w_out[i])
        h = h + conv_ffn(rms_norm(h, norm_ffn[i]), ffn_up[i], ffn_conv[i], ffn_down[i])
    return h[:, N_META:N_META + s]
```

```python
import functools
import math

import jax
import jax.numpy as jnp
from jax import lax
from jax.experimental import pallas as pl
from jax.experimental.pallas import tpu as pltpu

D_MODEL = 1024
N_META = 16
HEAD_DIM = 64
N_HEADS = 8
W_MIX = N_HEADS * HEAD_DIM
LORA_W = 64
LORA_A = 64
LORA_G = 128
D_FF = 2816
CONV_W = 3
NORM_EPS = 1e-6
GN_EPS = HEAD_DIM * 1e-5
DEPTH = 4

LANES = 128
SUBLANES = 8
VMEM_LIMIT = 56 * 1024 * 1024

C_Q = 0
C_K = C_Q + W_MIX
C_V = C_K + W_MIX
C_F = C_V + W_MIX
C_RW = C_F + LANES
RW_COLS = 3 * W_MIX + 3 * LANES
N_A = C_RW + RW_COLS
Z_W = 3 * W_MIX
Z_A = Z_W + LANES
Z_G = Z_A + LANES

RWKV_CHUNK = 64
EXP_M05 = math.exp(-0.5)

BF16 = jnp.bfloat16
F32 = jnp.float32


def _dot(a, b):
    return jnp.dot(a, b, preferred_element_type=F32)


def _dot_nt(a, b):
    return lax.dot_general(a, b, (((1,), (1,)), ((), ())), preferred_element_type=F32)


def _rms_norm_rows(x, gain):
    ms = jnp.mean(x * x, axis=-1, keepdims=True)
    return x * lax.rsqrt(ms + NORM_EPS) * gain


def _split3_bf16(x):
    h1 = x.astype(BF16)
    r1 = x - h1.astype(F32)
    h2 = r1.astype(BF16)
    h3 = (r1 - h2.astype(F32)).astype(BF16)
    return h1, h2, h3


def _tril_ones(n, dtype):
    r = lax.broadcasted_iota(jnp.int32, (n, n), 0)
    c = lax.broadcasted_iota(jnp.int32, (n, n), 1)
    return (r >= c).astype(dtype)


def _const_spec(shape):
    return pl.BlockSpec(shape, lambda *_: (0,) * len(shape), pipeline_mode=pl.Buffered(1))


def _in_kernel(h_ref, g_ref, w_ref, qkg_ref, bd_ref, fb_ref,
               q_ref, k_ref, v_ref, f_ref, rw_ref, carry_ref, *, tm):
    @pl.when(pl.program_id(1) == 0)
    def _():
        carry_ref[...] = jnp.zeros_like(carry_ref)

    hn = _rms_norm_rows(h_ref[...], g_ref[...]).astype(BF16)
    bd = bd_ref[...]

    def head_norm(t, gain):
        ms = _dot((t * t).astype(BF16), bd)
        return t * lax.rsqrt(ms + NORM_EPS) * gain

    q = _dot(hn, w_ref[:, C_Q:C_Q + W_MIX])
    q_ref[...] = (head_norm(q, qkg_ref[0:1, :]) * (HEAD_DIM ** -0.5)).astype(BF16)
    k = _dot(hn, w_ref[:, C_K:C_K + W_MIX])
    k_ref[...] = head_norm(k, qkg_ref[1:2, :]).astype(BF16)
    v_ref[...] = _dot(hn, w_ref[:, C_V:C_V + W_MIX]).astype(BF16)
    rw_ref[...] = _dot(hn, w_ref[:, C_RW:C_RW + RW_COLS])

    f = _dot(hn, w_ref[:, C_F:C_F + LANES]) + fb_ref[...]
    log_f = jnp.minimum(f, 0.0) - jnp.log1p(jnp.exp(-jnp.abs(f)))
    tri = _tril_ones(LANES, BF16)
    carry = carry_ref[0:1, :]
    for blk in range(tm // LANES):
        h1, h2, h3 = _split3_bf16(log_f[blk * LANES:(blk + 1) * LANES])
        c = _dot(tri, h1) + _dot(tri, h2) + _dot(tri, h3) + carry
        f_ref[:, blk * LANES:(blk + 1) * LANES] = c.T[0:N_HEADS, :]
        carry = c[LANES - 1:LANES, :]
    carry_ref[...] = jnp.broadcast_to(carry, carry_ref.shape)


def _in_call(h, gain, w_a, qk_gain, bd, f_bias, *, tm):
    b, lp, _ = h.shape
    grid = (b, lp // tm)
    row = lambda n: pl.BlockSpec((None, tm, n), lambda bi, i: (bi, i, 0))
    return pl.pallas_call(
        functools.partial(_in_kernel, tm=tm),
        grid=grid,
        in_specs=[
            row(D_MODEL),
            _const_spec((1, D_MODEL)),
            _const_spec((D_MODEL, N_A)),
            _const_spec((2, W_MIX)),
            _const_spec((W_MIX, W_MIX)),
            _const_spec((1, LANES)),
        ],
        out_specs=[
            row(W_MIX), row(W_MIX), row(W_MIX),
            pl.BlockSpec((None, N_HEADS, tm), lambda bi, i: (bi, 0, i)),
            row(RW_COLS),
        ],
        out_shape=[
            jax.ShapeDtypeStruct((b, lp, W_MIX), BF16),
            jax.ShapeDtypeStruct((b, lp, W_MIX), BF16),
            jax.ShapeDtypeStruct((b, lp, W_MIX), BF16),
            jax.ShapeDtypeStruct((b, N_HEADS, lp), F32),
            jax.ShapeDtypeStruct((b, lp, RW_COLS), F32),
        ],
        scratch_shapes=[pltpu.VMEM((SUBLANES, LANES), F32)],
        compiler_params=pltpu.CompilerParams(
            dimension_semantics=("parallel", "arbitrary"), vmem_limit_bytes=VMEM_LIMIT),
        name="in_proj",
    )(h, gain, w_a, qk_gain, bd, f_bias)


def _fox_kernel(q_ref, k_ref, v_ref, f_ref, o_ref, qs_ref, m_ref, l_ref, acc_ref, *, tq):
    i = pl.program_id(2)
    q = q_ref[...]
    lane = lax.broadcasted_iota(jnp.int32, q.shape, 1)
    first = lane < HEAD_DIM
    zero = jnp.zeros_like(q)
    qs_ref[0:tq, :] = jnp.where(first, q, zero)
    qs_ref[tq:2 * tq, :] = jnp.where(first, zero, q)
    m_ref[...] = jnp.full_like(m_ref, -jnp.inf)
    l_ref[...] = jnp.zeros_like(l_ref)
    acc_ref[...] = jnp.zeros_like(acc_ref)

    def step(j, diagonal):
        start = pl.multiple_of(j * tq, LANES)
        kb = k_ref[pl.ds(start, tq), :]
        vb = v_ref[pl.ds(start, tq), :]
        fk = f_ref[:, pl.ds(start, tq)]
        s = _dot_nt(qs_ref[...], kb)
        ps = []
        for hh in range(2):
            rows = slice(hh * tq, (hh + 1) * tq)
            sh = s[rows] - fk[hh:hh + 1, :]
            if diagonal:
                r = lax.broadcasted_iota(jnp.int32, sh.shape, 0)
                c = lax.broadcasted_iota(jnp.int32, sh.shape, 1)
                sh = jnp.where(c <= r, sh, -jnp.inf)
            m_prev = m_ref[rows, :]
            m_new = jnp.maximum(m_prev, jnp.max(sh, axis=-1, keepdims=True))
            alpha = jnp.exp(m_prev - m_new)
            p = jnp.exp(sh - m_new)
            l_ref[rows, :] = alpha * l_ref[rows, :] + jnp.sum(p, axis=-1, keepdims=True)
            acc_ref[rows, :] = alpha * acc_ref[rows, :]
            m_ref[rows, :] = m_new
            ps.append(p.astype(BF16))
        acc_ref[...] += _dot(jnp.concatenate(ps, axis=0), vb)

    def body(j, carry):
        step(j, False)
        return carry

    lax.fori_loop(0, i, body, 0)
    step(i, True)

    out = acc_ref[...] / l_ref[...]
    o_ref[...] = jnp.where(first, out[0:tq], out[tq:2 * tq]).astype(BF16)


def _fox_call(q, k, v, fcum, *, tq):
    b, lp, _ = q.shape
    n_pairs = W_MIX // LANES
    fpair = fcum.reshape(b * n_pairs, 2, lp)
    grid = (b, n_pairs, lp // tq)
    return pl.pallas_call(
        functools.partial(_fox_kernel, tq=tq),
        grid=grid,
        in_specs=[
            pl.BlockSpec((None, tq, LANES), lambda bi, hp, i: (bi, i, hp)),
            pl.BlockSpec((None, lp, LANES), lambda bi, hp, i: (bi, 0, hp)),
            pl.BlockSpec((None, lp, LANES), lambda bi, hp, i: (bi, 0, hp)),
            pl.BlockSpec((None, 2, lp), lambda bi, hp, i: (bi * n_pairs + hp, 0, 0)),
        ],
        out_specs=pl.BlockSpec((None, tq, LANES), lambda bi, hp, i: (bi, i, hp)),
        out_shape=jax.ShapeDtypeStruct((b, lp, W_MIX), BF16),
        scratch_shapes=[
            pltpu.VMEM((2 * tq, LANES), BF16),
            pltpu.VMEM((2 * tq, 1), F32),
            pltpu.VMEM((2 * tq, 1), F32),
            pltpu.VMEM((2 * tq, LANES), F32),
        ],
        compiler_params=pltpu.CompilerParams(
            dimension_semantics=("parallel", "parallel", "arbitrary"),
            vmem_limit_bytes=VMEM_LIMIT),
        name="fox_attention",
    )(q, k, v, fpair)


def _rwkv_kernel(rw_ref, mu_ref, wup_ref, aup_ref, gup_ref, p_ref, o_ref, st_ref, prev_ref):
    cs = RWKV_CHUNK

    @pl.when(pl.program_id(1) == 0)
    def _():
        st_ref[...] = jnp.zeros_like(st_ref)
        prev_ref[...] = jnp.zeros_like(prev_ref)

    rw = rw_ref[...]
    row = lax.broadcasted_iota(jnp.int32, rw.shape, 0)
    rw_prev = jnp.where(row == 0, prev_ref[SUBLANES - 1:SUBLANES, :], pltpu.roll(rw, 1, 0))
    prev_ref[...] = rw[cs - SUBLANES:cs, :]
    z = rw + mu_ref[...] * (rw_prev - rw)

    r_all = z[:, 0:W_MIX]
    kr = z[:, W_MIX:2 * W_MIX]
    v_all = z[:, 2 * W_MIX:3 * W_MIX]
    zw = z[:, Z_W:Z_W + LANES]
    za = z[:, Z_A:Z_A + LANES]
    zg = z[:, Z_G:Z_G + LANES]
    prm = p_ref[...]
    w0, a0, k_k, k_a = prm[0:1], prm[1:2], prm[2:3], prm[3:4]
    r_k, gn_w, gn_b = prm[4:5], prm[5:6], prm[6:7]

    lw = -EXP_M05 * jax.nn.sigmoid(w0 + _dot(jnp.tanh(zw).astype(BF16), wup_ref[...]))
    a_rate = jax.nn.sigmoid(a0 + _dot(za.astype(BF16), aup_ref[...]))
    g_all = _dot(jax.nn.sigmoid(zg).astype(BF16), gup_ref[...])
    kk_all = kr * k_k
    k_all = kr * (1.0 + (a_rate - 1.0) * k_a)

    tri = _tril_ones(cs, BF16)
    l1, l2, l3 = _split3_bf16(lw)
    cw = _dot(tri, l1) + _dot(tri, l2) + _dot(tri, l3)
    cw_last = cw[cs - 1:cs, :]
    c_ref = 0.5 * cw_last
    e_cw = jnp.exp(cw)
    e_cwm = jnp.exp(cw - lw)
    e_nref = jnp.exp(-c_ref)
    e_neg = jnp.exp(c_ref - cw)
    e_last = jnp.exp(cw_last - cw)
    e_total = jnp.exp(cw_last)

    ri = lax.broadcasted_iota(jnp.int32, (cs, cs), 0)
    ci = lax.broadcasted_iota(jnp.int32, (cs, cs), 1)
    strict = ri > ci
    incl = ri >= ci

    for h in range(N_HEADS):
        sl = slice(h * HEAD_DIM, (h + 1) * HEAD_DIM)
        kk = kk_all[:, sl]
        kk = kk / jnp.maximum(jnp.sqrt(jnp.sum(kk * kk, axis=-1, keepdims=True)), 1e-12)
        a_h = -kk
        b_h = kk * a_rate[:, sl]
        r_h, k_h, v_h = r_all[:, sl], k_all[:, sl], v_all[:, sl]
        vb = v_h.astype(BF16)

        lhs = jnp.concatenate([a_h * (e_cwm[:, sl] * e_nref[:, sl]),
                               r_h * (e_cw[:, sl] * e_nref[:, sl])], axis=0).astype(BF16)
        m_b = _dot_nt(lhs, (b_h * e_neg[:, sl]).astype(BF16))
        m_k = _dot_nt(lhs, (k_h * e_neg[:, sl]).astype(BF16))
        a_ab = jnp.where(strict, m_b[0:cs], 0.0)
        a_ak = jnp.where(strict, m_k[0:cs], 0.0)
        a_rb = jnp.where(incl, m_b[cs:2 * cs], 0.0)
        a_rk = jnp.where(incl, m_k[cs:2 * cs], 0.0)

        s0 = st_ref[h]
        lhs0 = jnp.concatenate([a_h * e_cwm[:, sl], r_h * e_cw[:, sl]], axis=0).astype(BF16)
        s_in = _dot_nt(lhs0, s0.astype(BF16))
        x = s_in[0:cs] + _dot(a_ak.astype(BF16), vb)

        pw = a_ab
        x = x + _dot(pw.astype(BF16), x.astype(BF16))
        for _ in range(int(math.log2(cs)) - 1):
            pwb = pw.astype(BF16)
            pw = _dot(pwb, pwb)
            x = x + _dot(pw.astype(BF16), x.astype(BF16))
        ub = x.astype(BF16)

        y = s_in[cs:2 * cs] + _dot(a_rb.astype(BF16), ub) + _dot(a_rk.astype(BF16), vb)

        st_ref[h] = (s0 * e_total[:, sl]
                     + _dot(x.T.astype(BF16), (b_h * e_last[:, sl]).astype(BF16))
                     + _dot(v_h.T.astype(BF16), (k_h * e_last[:, sl]).astype(BF16)))

        mean = jnp.mean(y, axis=-1, keepdims=True)
        d = y - mean
        var = jnp.mean(d * d, axis=-1, keepdims=True)
        yn = d * lax.rsqrt(var + GN_EPS) * gn_w[:, sl] + gn_b[:, sl]
        bonus = jnp.sum(r_h * k_h * r_k[:, sl], axis=-1, keepdims=True) * v_h
        o_ref[:, sl] = ((yn + bonus) * g_all[:, sl]).astype(BF16)


def _rwkv_call(rw, mu, w_up, a_up, g_up, prm):
    b, lp, _ = rw.shape
    cs = RWKV_CHUNK
    return pl.pallas_call(
        _rwkv_kernel,
        grid=(b, lp // cs),
        in_specs=[
            pl.BlockSpec((None, cs, RW_COLS), lambda bi, c: (bi, c, 0)),
            _const_spec((1, RW_COLS)),
            _const_spec((LANES, W_MIX)),
            _const_spec((LANES, W_MIX)),
            _const_spec((LANES, W_MIX)),
            _const_spec((SUBLANES, W_MIX)),
        ],
        out_specs=pl.BlockSpec((None, cs, W_MIX), lambda bi, c: (bi, c, 0)),
        out_shape=jax.ShapeDtypeStruct((b, lp, W_MIX), BF16),
        scratch_shapes=[
            pltpu.VMEM((N_HEADS, HEAD_DIM, HEAD_DIM), F32),
            pltpu.VMEM((SUBLANES, RW_COLS), F32),
        ],
        compiler_params=pltpu.CompilerParams(
            dimension_semantics=("parallel", "arbitrary"), vmem_limit_bytes=VMEM_LIMIT),
        name="rwkv7",
    )(rw, mu, w_up, a_up, g_up, prm)


def _merge_kernel(h_ref, g_ref, of_ref, or_ref, wg_ref, gb_ref, wbf_ref, wbr_ref, wo_ref, out_ref):
    x = h_ref[...]
    hn = _rms_norm_rows(x, g_ref[...]).astype(BF16)
    gates = jax.nn.sigmoid(_dot(hn, wg_ref[...]) + gb_ref[...])
    merged = (gates[:, 0:D_MODEL] * _dot(of_ref[...], wbf_ref[...])
              + gates[:, D_MODEL:2 * D_MODEL] * _dot(or_ref[...], wbr_ref[...]))
    out_ref[...] = x + _dot(merged.astype(BF16), wo_ref[...])


def _merge_call(h, gain, o_fox, o_rwkv, w_g, g_bias, w_bf, w_br, w_o, *, tm):
    b, lp, _ = h.shape
    row = lambda n: pl.BlockSpec((None, tm, n), lambda bi, i: (bi, i, 0))
    return pl.pallas_call(
        _merge_kernel,
        grid=(b, lp // tm),
        in_specs=[
            row(D_MODEL), _const_spec((1, D_MODEL)), row(W_MIX), row(W_MIX),
            _const_spec((D_MODEL, 2 * D_MODEL)), _const_spec((1, 2 * D_MODEL)),
            _const_spec((W_MIX, D_MODEL)), _const_spec((W_MIX, D_MODEL)),
            _const_spec((D_MODEL, D_MODEL)),
        ],
        out_specs=row(D_MODEL),
        out_shape=jax.ShapeDtypeStruct(h.shape, F32),
        compiler_params=pltpu.CompilerParams(
            dimension_semantics=("parallel", "parallel"), vmem_limit_bytes=VMEM_LIMIT),
        name="merge",
    )(h, gain, o_fox, o_rwkv, w_g, g_bias, w_bf, w_br, w_o)


FF_CHUNK = D_FF // 2


def _ffn_kernel(h_ref, g_ref, wup_ref, wc_ref, wd_ref, out_ref, carry_ref, *, tm):
    @pl.when(pl.program_id(1) == 0)
    def _():
        carry_ref[...] = jnp.zeros_like(carry_ref)

    x = h_ref[...]
    hn = _rms_norm_rows(x, g_ref[...]).astype(BF16)
    wc = wc_ref[...]
    row = lax.broadcasted_iota(jnp.int32, (tm, FF_CHUNK), 0)

    def conv(u, col):
        prev = carry_ref[:, col:col + FF_CHUNK]
        p1 = prev[SUBLANES - 1:SUBLANES, :]
        p2 = prev[SUBLANES - 2:SUBLANES - 1, :]
        u1 = jnp.where(row == 0, p1, pltpu.roll(u, 1, 0))
        u2 = jnp.where(row == 0, p2, jnp.where(row == 1, p1, pltpu.roll(u, 2, 0)))
        carry_ref[:, col:col + FF_CHUNK] = u[tm - SUBLANES:tm, :]
        w = wc[:, col:col + FF_CHUNK]
        return w[0:1] * u2 + w[1:2] * u1 + w[2:3] * u

    acc = x
    for c in range(D_FF // FF_CHUNK):
        cg = c * FF_CHUNK
        cv = D_FF + c * FF_CHUNK
        gate = conv(_dot(hn, wup_ref[:, cg:cg + FF_CHUNK]), cg)
        val = conv(_dot(hn, wup_ref[:, cv:cv + FF_CHUNK]), cv)
        act = (gate * jax.nn.sigmoid(gate) * val).astype(BF16)
        acc = acc + _dot(act, wd_ref[cg:cg + FF_CHUNK, :])
    out_ref[...] = acc


def _ffn_call(h, gain, w_up, w_conv, w_down, *, tm):
    b, lp, _ = h.shape
    row = pl.BlockSpec((None, tm, D_MODEL), lambda bi, i: (bi, i, 0))
    return pl.pallas_call(
        functools.partial(_ffn_kernel, tm=tm),
        grid=(b, lp // tm),
        in_specs=[
            row, _const_spec((1, D_MODEL)),
            _const_spec((D_MODEL, 2 * D_FF)), _const_spec((CONV_W, 2 * D_FF)),
            _const_spec((D_FF, D_MODEL)),
        ],
        out_specs=row,
        out_shape=jax.ShapeDtypeStruct(h.shape, F32),
        scratch_shapes=[pltpu.VMEM((SUBLANES, 2 * D_FF), F32)],
        compiler_params=pltpu.CompilerParams(
            dimension_semantics=("parallel", "arbitrary"), vmem_limit_bytes=VMEM_LIMIT),
        name="conv_ffn",
    )(h, gain, w_up, w_conv, w_down)


def _row_tile(lp):
    for t in (640, 512, 384, 256, 128):
        if lp % t == 0:
            return t
    raise ValueError(f"padded length {lp} has no supported row tile")


def _pad_cols(w, n):
    return jnp.pad(w, [(0, 0)] * (w.ndim - 1) + [(0, n - w.shape[-1])])


def _prep_in_weights(w_in, f_bias, mu):
    o_f = 3 * W_MIX
    o_rw = o_f + N_HEADS
    n_rw = 3 * W_MIX + LORA_W + LORA_A + LORA_G
    o_g = o_rw + n_rw
    rw = w_in[..., o_rw:o_g]

    def rw_layout(t):
        return jnp.concatenate([
            t[..., 0:3 * W_MIX],
            _pad_cols(t[..., 3 * W_MIX:3 * W_MIX + LORA_W], LANES),
            _pad_cols(t[..., 3 * W_MIX + LORA_W:3 * W_MIX + LORA_W + LORA_A], LANES),
            t[..., 3 * W_MIX + LORA_W + LORA_A:],
        ], axis=-1)

    w_a = jnp.concatenate([
        w_in[..., 0:o_f],
        _pad_cols(w_in[..., o_f:o_rw], LANES),
        rw_layout(rw),
    ], axis=-1).astype(BF16)
    w_g = w_in[..., o_g:].astype(BF16)
    fb = _pad_cols(f_bias, LANES)[:, None, :]
    return w_a, w_g, fb, rw_layout(mu)[:, None, :]


def kernel(x, meta_tokens, norm_mix, norm_ffn, w_in, fox_q_norm, fox_k_norm, fox_f_bias,
           rwkv_shift_mu, rwkv_w0, rwkv_w_up, rwkv_a0, rwkv_a_up, rwkv_g_up, rwkv_k_k,
           rwkv_k_a, rwkv_r_k, rwkv_gn_w, rwkv_gn_b, w_branch_fox, w_branch_rwkv,
           gate_bias, w_out, ffn_up, ffn_conv, ffn_down):
    b, s, _ = x.shape
    depth = w_in.shape[0]
    length = N_META + s
    lp = -(-length // LANES) * LANES
    tm = _row_tile(lp)

    meta = jnp.broadcast_to(meta_tokens.astype(x.dtype)[None], (b, N_META, D_MODEL))
    h = jnp.concatenate([meta, x], axis=1)
    h = jnp.pad(h, ((0, 0), (0, lp - length), (0, 0)))

    w_a, w_g, f_b, mu = _prep_in_weights(w_in, fox_f_bias, rwkv_shift_mu)
    qk_gain = jnp.stack([jnp.tile(fox_q_norm, (1, N_HEADS)), jnp.tile(fox_k_norm, (1, N_HEADS))], axis=1)
    bd = (jnp.kron(jnp.eye(N_HEADS, dtype=F32), jnp.ones((HEAD_DIM, HEAD_DIM), F32)) / HEAD_DIM).astype(BF16)
    pad_rows = lambda w: jnp.pad(w, ((0, 0), (0, LANES - w.shape[1]), (0, 0))).astype(BF16)
    w_up_l, a_up_l, g_up_l = pad_rows(rwkv_w_up), pad_rows(rwkv_a_up), rwkv_g_up.astype(BF16)
    prm = jnp.stack([rwkv_w0, rwkv_a0, rwkv_k_k, rwkv_k_a, rwkv_r_k.reshape(depth, W_MIX),
                     rwkv_gn_w, rwkv_gn_b, jnp.zeros_like(rwkv_w0)], axis=1)
    w_bf, w_br, w_o = w_branch_fox.astype(BF16), w_branch_rwkv.astype(BF16), w_out.astype(BF16)
    f_up, f_down = ffn_up.astype(BF16), ffn_down.astype(BF16)
    g_mix, g_ffn, g_bias = norm_mix[:, None, :], norm_ffn[:, None, :], gate_bias[:, None, :]

    for i in range(depth):
        q, k, v, fcum, rw = _in_call(h, g_mix[i], w_a[i], qk_gain[i], bd, f_b[i], tm=tm)
        o_fox = _fox_call(q, k, v, fcum, tq=tm)
        o_rwkv = _rwkv_call(rw, mu[i], w_up_l[i], a_up_l[i], g_up_l[i], prm[i])
        h = _merge_call(h, g_mix[i], o_fox, o_rwkv, w_g[i], g_bias[i], w_bf[i], w_br[i], w_o[i], tm=tm)
        h = _ffn_call(h, g_ffn[i], f_up[i], ffn_conv[i], f_down[i], tm=tm)
    return h[:, N_META:N_META + s]
```

```python
import functools
import math

import jax
import jax.numpy as jnp
from jax import lax
from jax.experimental import pallas as pl
from jax.experimental.pallas import tpu as pltpu

D_MODEL = 1024
N_META = 16
HEAD_DIM = 64
N_HEADS = 8
W_MIX = N_HEADS * HEAD_DIM
LORA_W = 64
LORA_A = 64
LORA_G = 128
D_FF = 2816
CONV_W = 3
NORM_EPS = 1e-6
GN_EPS = HEAD_DIM * 1e-5
DEPTH = 4

LANES = 128
SUBLANES = 8
VMEM_LIMIT = 56 * 1024 * 1024

C_Q = 0
C_K = C_Q + W_MIX
C_V = C_K + W_MIX
C_F = C_V + W_MIX
C_RW = C_F + LANES
RW_COLS = 3 * W_MIX + 3 * LANES
N_A = C_RW + RW_COLS
Z_W = 3 * W_MIX
Z_A = Z_W + LANES
Z_G = Z_A + LANES

RWKV_CHUNK = 64
EXP_M05 = math.exp(-0.5)
LOG2E = math.log2(math.e)
FOX_CHAINS = 8
FOX_LOOKAHEAD = 2

BF16 = jnp.bfloat16
F32 = jnp.float32


def _dot(a, b):
    return jnp.dot(a, b, preferred_element_type=F32)


def _dot_nt(a, b):
    return lax.dot_general(a, b, (((1,), (1,)), ((), ())), preferred_element_type=F32)


def _rms_norm_rows(x, gain):
    ms = jnp.mean(x * x, axis=-1, keepdims=True)
    return x * lax.rsqrt(ms + NORM_EPS) * gain


def _split3_bf16(x):
    h1 = x.astype(BF16)
    r1 = x - h1.astype(F32)
    h2 = r1.astype(BF16)
    h3 = (r1 - h2.astype(F32)).astype(BF16)
    return h1, h2, h3


def _tril_ones(n, dtype):
    r = lax.broadcasted_iota(jnp.int32, (n, n), 0)
    c = lax.broadcasted_iota(jnp.int32, (n, n), 1)
    return (r >= c).astype(dtype)


def _const_spec(shape):
    return pl.BlockSpec(shape, lambda *_: (0,) * len(shape), pipeline_mode=pl.Buffered(1))


def _in_kernel(h_ref, g_ref, w_ref, qkg_ref, bd_ref, fb_ref,
               q_ref, k_ref, v_ref, f_ref, rw_ref, carry_ref, *, tm):
    @pl.when(pl.program_id(1) == 0)
    def _():
        carry_ref[...] = jnp.zeros_like(carry_ref)

    hn = _rms_norm_rows(h_ref[...], g_ref[...]).astype(BF16)
    bd = bd_ref[...]

    def head_norm(t, gain):
        ms = _dot((t * t).astype(BF16), bd)
        return t * lax.rsqrt(ms + NORM_EPS) * gain

    q = _dot(hn, w_ref[:, C_Q:C_Q + W_MIX])
    q_ref[...] = (head_norm(q, qkg_ref[0:1, :]) * (HEAD_DIM ** -0.5 * LOG2E)).astype(BF16)
    k = _dot(hn, w_ref[:, C_K:C_K + W_MIX])
    k_ref[...] = head_norm(k, qkg_ref[1:2, :]).astype(BF16)
    v_ref[...] = _dot(hn, w_ref[:, C_V:C_V + W_MIX]).astype(BF16)
    rw_ref[...] = _dot(hn, w_ref[:, C_RW:C_RW + RW_COLS])

    f = _dot(hn, w_ref[:, C_F:C_F + LANES]) + fb_ref[...]
    log_f = jnp.minimum(f, 0.0) - jnp.log1p(jnp.exp(-jnp.abs(f)))
    tri = _tril_ones(LANES, BF16)
    carry = carry_ref[0:1, :]
    for blk in range(tm // LANES):
        h1, h2, h3 = _split3_bf16(log_f[blk * LANES:(blk + 1) * LANES])
        c = _dot(tri, h1) + _dot(tri, h2) + _dot(tri, h3) + carry
        f_ref[:, blk * LANES:(blk + 1) * LANES] = c.T[0:N_HEADS, :] * LOG2E
        carry = c[LANES - 1:LANES, :]
    carry_ref[...] = jnp.broadcast_to(carry, carry_ref.shape)


def _in_call(h, gain, w_a, qk_gain, bd, f_bias, *, tm):
    b, lp, _ = h.shape
    grid = (b, lp // tm)
    row = lambda n: pl.BlockSpec((None, tm, n), lambda bi, i: (bi, i, 0))
    return pl.pallas_call(
        functools.partial(_in_kernel, tm=tm),
        grid=grid,
        in_specs=[
            row(D_MODEL),
            _const_spec((1, D_MODEL)),
            _const_spec((D_MODEL, N_A)),
            _const_spec((2, W_MIX)),
            _const_spec((W_MIX, W_MIX)),
            _const_spec((1, LANES)),
        ],
        out_specs=[
            row(W_MIX), row(W_MIX), row(W_MIX),
            pl.BlockSpec((None, N_HEADS, tm), lambda bi, i: (bi, 0, i)),
            row(RW_COLS),
        ],
        out_shape=[
            jax.ShapeDtypeStruct((b, lp, W_MIX), BF16),
            jax.ShapeDtypeStruct((b, lp, W_MIX), BF16),
            jax.ShapeDtypeStruct((b, lp, W_MIX), BF16),
            jax.ShapeDtypeStruct((b, N_HEADS, lp), F32),
            jax.ShapeDtypeStruct((b, lp, RW_COLS), F32),
        ],
        scratch_shapes=[pltpu.VMEM((SUBLANES, LANES), F32)],
        compiler_params=pltpu.CompilerParams(
            dimension_semantics=("parallel", "arbitrary"), vmem_limit_bytes=VMEM_LIMIT),
        name="in_proj",
    )(h, gain, w_a, qk_gain, bd, f_bias)


def _fox_kernel(q_ref, k_ref, v_ref, f_ref, o_ref, qs_ref, m_ref, l_ref, acc_ref, *, tq):
    i = pl.program_id(2)
    q = q_ref[...]
    lane = lax.broadcasted_iota(jnp.int32, q.shape, 1)
    first = lane < HEAD_DIM
    zero = jnp.zeros_like(q)
    qs_ref[0:tq, :] = jnp.where(first, q, zero)
    qs_ref[tq:2 * tq, :] = jnp.where(first, zero, q)
    m_ref[...] = jnp.full_like(m_ref, -jnp.inf)
    l_ref[...] = jnp.zeros_like(l_ref)
    acc_ref[...] = jnp.zeros_like(acc_ref)

    rc = 2 * tq // FOX_CHAINS

    def step(js, diagonal):
        blocks = []
        for j in js:
            start = pl.multiple_of(j * tq, LANES)
            blocks.append((k_ref[pl.ds(start, tq), :], v_ref[pl.ds(start, tq), :],
                           f_ref[:, pl.ds(start, tq)]))
        items = [(b, c) for b in range(len(js)) for c in range(FOX_CHAINS)]
        scores = {}

        def issue(n):
            b, c = items[n]
            scores[n] = _dot_nt(qs_ref[c * rc:(c + 1) * rc, :], blocks[b][0])

        for n in range(min(FOX_LOOKAHEAD, len(items))):
            issue(n)
        for n, (b, c) in enumerate(items):
            if n + FOX_LOOKAHEAD < len(items):
                issue(n + FOX_LOOKAHEAD)
            _, vb, fk = blocks[b]
            rows = slice(c * rc, (c + 1) * rc)
            hh = (c * rc) // tq
            sh = scores.pop(n) - fk[hh:hh + 1, :]
            if diagonal:
                r = lax.broadcasted_iota(jnp.int32, sh.shape, 0) + (c * rc) % tq
                cc = lax.broadcasted_iota(jnp.int32, sh.shape, 1)
                sh = jnp.where(cc <= r, sh, -jnp.inf)
            m_prev = m_ref[rows, :]
            m_new = jnp.maximum(m_prev, jnp.max(sh, axis=-1, keepdims=True))
            alpha = jnp.exp2(m_prev - m_new)
            p = jnp.exp2(sh - m_new)
            l_ref[rows, :] = alpha * l_ref[rows, :] + jnp.sum(p, axis=-1, keepdims=True)
            m_ref[rows, :] = m_new
            acc_ref[rows, :] = alpha * acc_ref[rows, :] + _dot(p.astype(BF16), vb)

    def pair_body(jj, carry):
        step([2 * jj, 2 * jj + 1], False)
        return carry

    lax.fori_loop(0, lax.shift_right_logical(i, 1), pair_body, 0)

    @pl.when((i & 1) == 1)
    def _():
        step([i - 1], False)

    step([i], True)

    out = acc_ref[...] / l_ref[...]
    o_ref[...] = jnp.where(first, out[0:tq], out[tq:2 * tq]).astype(BF16)


def _fox_call(q, k, v, fcum, *, tq):
    b, lp, _ = q.shape
    n_pairs = W_MIX // LANES
    fpair = fcum.reshape(b * n_pairs, 2, lp)
    grid = (b, n_pairs, lp // tq)
    return pl.pallas_call(
        functools.partial(_fox_kernel, tq=tq),
        grid=grid,
        in_specs=[
            pl.BlockSpec((None, tq, LANES), lambda bi, hp, i: (bi, i, hp)),
            pl.BlockSpec((None, lp, LANES), lambda bi, hp, i: (bi, 0, hp)),
            pl.BlockSpec((None, lp, LANES), lambda bi, hp, i: (bi, 0, hp)),
            pl.BlockSpec((None, 2, lp), lambda bi, hp, i: (bi * n_pairs + hp, 0, 0)),
        ],
        out_specs=pl.BlockSpec((None, tq, LANES), lambda bi, hp, i: (bi, i, hp)),
        out_shape=jax.ShapeDtypeStruct((b, lp, W_MIX), BF16),
        scratch_shapes=[
            pltpu.VMEM((2 * tq, LANES), BF16),
            pltpu.VMEM((2 * tq, 1), F32),
            pltpu.VMEM((2 * tq, 1), F32),
            pltpu.VMEM((2 * tq, LANES), F32),
        ],
        compiler_params=pltpu.CompilerParams(
            dimension_semantics=("parallel", "parallel", "arbitrary"),
            vmem_limit_bytes=VMEM_LIMIT),
        name="fox_attention",
    )(q, k, v, fpair)


def _rwkv_kernel(rw_ref, mu_ref, wup_ref, aup_ref, gup_ref, p_ref, o_ref, st_ref, prev_ref):
    cs = RWKV_CHUNK
    n_batch = rw_ref.shape[0]
    n_pairs = W_MIX // LANES

    @pl.when(pl.program_id(0) == 0)
    def _():
        st_ref[...] = jnp.zeros_like(st_ref)
        prev_ref[...] = jnp.zeros_like(prev_ref)

    prm = p_ref[...]
    w0, a0, k_k, k_a = prm[0:1], prm[1:2], prm[2:3], prm[3:4]
    r_k, gn_w, gn_b = prm[4:5], prm[5:6], prm[6:7]
    tri = _tril_ones(cs, BF16)

    lane = lax.broadcasted_iota(jnp.int32, (cs, LANES), 1)
    rowi = lax.broadcasted_iota(jnp.int32, (cs, LANES), 0)
    lo = lane < HEAD_DIM
    hi = jnp.logical_not(lo)
    col = lane & (HEAD_DIM - 1)
    strict = rowi > col
    incl = rowi >= col
    zeros_blk = jnp.zeros((cs, LANES), BF16)

    def half_sums(t):
        s_lo = jnp.sum(jnp.where(lo, t, 0.0), axis=-1, keepdims=True)
        s_hi = jnp.sum(jnp.where(hi, t, 0.0), axis=-1, keepdims=True)
        return jnp.where(lo, s_lo, s_hi)

    per_b = []
    for bi in range(n_batch):
        rw = rw_ref[bi]
        row = lax.broadcasted_iota(jnp.int32, rw.shape, 0)
        rw_prev = jnp.where(row == 0, prev_ref[bi, SUBLANES - 1:SUBLANES, :], pltpu.roll(rw, 1, 0))
        prev_ref[bi] = rw[cs - SUBLANES:cs, :]
        z = rw + mu_ref[...] * (rw_prev - rw)
        r_all = z[:, 0:W_MIX]
        kr = z[:, W_MIX:2 * W_MIX]
        v_all = z[:, 2 * W_MIX:3 * W_MIX]
        zw = z[:, Z_W:Z_W + LANES]
        za = z[:, Z_A:Z_A + LANES]
        zg = z[:, Z_G:Z_G + LANES]

        lw = -EXP_M05 * jax.nn.sigmoid(w0 + _dot(jnp.tanh(zw).astype(BF16), wup_ref[...]))
        a_rate = jax.nn.sigmoid(a0 + _dot(za.astype(BF16), aup_ref[...]))
        g_all = _dot(jax.nn.sigmoid(zg).astype(BF16), gup_ref[...])
        k_all = kr * (1.0 + (a_rate - 1.0) * k_a)
        kk = kr * k_k
        kk = jnp.concatenate(
            [kk[:, p * LANES:(p + 1) * LANES]
             / jnp.maximum(jnp.sqrt(half_sums(jnp.square(kk[:, p * LANES:(p + 1) * LANES]))), 1e-12)
             for p in range(n_pairs)], axis=1)
        b_all = kk * a_rate

        l1, l2, l3 = _split3_bf16(lw)
        cw = _dot(tri, l1) + _dot(tri, l2) + _dot(tri, l3)
        cw_last = cw[cs - 1:cs, :]
        c_ref = 0.5 * cw_last
        e_cw = jnp.exp(cw)
        e_cwm = jnp.exp(cw - lw)
        e_nref = jnp.exp(-c_ref)
        e_neg = jnp.exp(c_ref - cw)
        e_last = jnp.exp(cw_last - cw)
        per_b.append(dict(
            r=r_all, k=k_all, v=v_all, g=g_all,
            v_roll=pltpu.roll(v_all, HEAD_DIM, 1),
            at=-kk * (e_cwm * e_nref), rt=r_all * (e_cw * e_nref),
            bt=b_all * e_neg, kt=k_all * e_neg,
            a0=-kk * e_cwm, r0=r_all * e_cw,
            bh=b_all * e_last, kh=k_all * e_last,
            e_total=jnp.exp(cw_last)))

    heads = [(bi, h) for bi in range(n_batch) for h in range(N_HEADS)]

    def grp(bi, h, name):
        p = h // 2
        return per_b[bi][name][:, p * LANES:(p + 1) * LANES]

    def mine(h):
        return hi if h % 2 else lo

    def stack_masked(bi, h, top, bot):
        t = jnp.concatenate([grp(bi, h, top), grp(bi, h, bot)], axis=0)
        m = jnp.concatenate([mine(h), mine(h)], axis=0)
        return jnp.where(m, t, 0.0).astype(BF16)

    m_all = [_dot_nt(stack_masked(bi, h, "at", "rt"),
                     jnp.concatenate([grp(bi, h, "bt"), grp(bi, h, "kt")], axis=0).astype(BF16))
             for bi, h in heads]
    m_top = [jnp.where(strict, m[0:cs], 0.0) for m in m_all]
    m_bot = [jnp.where(incl, m[cs:2 * cs], 0.0).astype(BF16) for m in m_all]
    v_hi = [jnp.where(hi, grp(bi, h, "v" if h % 2 else "v_roll"), 0.0) for bi, h in heads]

    s_old = [st_ref[i] for i in range(len(heads))]
    s_in = [_dot_nt(stack_masked(bi, h, "a0", "r0"), s_old[i].astype(BF16))
            for i, (bi, h) in enumerate(heads)]

    w = [jnp.where(lo, m_top[i],
                   s_in[i][0:cs] + _dot(m_top[i].astype(BF16),
                                        jnp.concatenate([zeros_blk, v_hi[i].astype(BF16)], axis=0)))
         for i in range(len(heads))]
    for _ in range(int(math.log2(cs))):
        pw = [_dot(w[i][:, 0:HEAD_DIM].astype(BF16), w[i].astype(BF16)) for i in range(len(heads))]
        w = [pw[i] + jnp.where(lo, 0.0, w[i]) for i in range(len(heads))]

    z32 = [jnp.concatenate([jnp.where(lo, 0.0, w[i]), v_hi[i]], axis=0) for i in range(len(heads))]
    y = [s_in[i][cs:2 * cs] + _dot(m_bot[i], z32[i].astype(BF16)) for i in range(len(heads))]

    for i, (bi, h) in enumerate(heads):
        st_ref[i] = (s_old[i] * grp(bi, h, "e_total")
                     + _dot(z32[i].T.astype(BF16), stack_masked(bi, h, "bh", "kh")))

    for bi in range(n_batch):
        for p in range(n_pairs):
            i_even = bi * N_HEADS + 2 * p
            sl = slice(p * LANES, (p + 1) * LANES)
            yp = pltpu.roll(y[i_even], HEAD_DIM, 1) + y[i_even + 1]
            mean = half_sums(yp) * (1.0 / HEAD_DIM)
            d = yp - mean
            var = half_sums(d * d) * (1.0 / HEAD_DIM)
            yn = d * lax.rsqrt(var + GN_EPS) * gn_w[:, sl] + gn_b[:, sl]
            pb = per_b[bi]
            bonus = half_sums(pb["r"][:, sl] * pb["k"][:, sl] * r_k[:, sl]) * pb["v"][:, sl]
            o_ref[bi, :, sl] = ((yn + bonus) * pb["g"][:, sl]).astype(BF16)


def _rwkv_call(rw, mu, w_up, a_up, g_up, prm):
    b, lp, _ = rw.shape
    cs = RWKV_CHUNK
    return pl.pallas_call(
        _rwkv_kernel,
        grid=(lp // cs,),
        in_specs=[
            pl.BlockSpec((b, cs, RW_COLS), lambda c: (0, c, 0)),
            _const_spec((1, RW_COLS)),
            _const_spec((LANES, W_MIX)),
            _const_spec((LANES, W_MIX)),
            _const_spec((LANES, W_MIX)),
            _const_spec((SUBLANES, W_MIX)),
        ],
        out_specs=pl.BlockSpec((b, cs, W_MIX), lambda c: (0, c, 0)),
        out_shape=jax.ShapeDtypeStruct((b, lp, W_MIX), BF16),
        scratch_shapes=[
            pltpu.VMEM((b * N_HEADS, 2 * cs, LANES), F32),
            pltpu.VMEM((b, SUBLANES, RW_COLS), F32),
        ],
        compiler_params=pltpu.CompilerParams(
            dimension_semantics=("arbitrary",), vmem_limit_bytes=VMEM_LIMIT),
        name="rwkv7",
    )(rw, mu, w_up, a_up, g_up, prm)


def _merge_kernel(h_ref, g_ref, of_ref, or_ref, wg_ref, gb_ref, wbf_ref, wbr_ref, wo_ref, out_ref):
    x = h_ref[...]
    hn = _rms_norm_rows(x, g_ref[...]).astype(BF16)
    gates = jax.nn.sigmoid(_dot(hn, wg_ref[...]) + gb_ref[...])
    merged = (gates[:, 0:D_MODEL] * _dot(of_ref[...], wbf_ref[...])
              + gates[:, D_MODEL:2 * D_MODEL] * _dot(or_ref[...], wbr_ref[...]))
    out_ref[...] = x + _dot(merged.astype(BF16), wo_ref[...])


def _merge_call(h, gain, o_fox, o_rwkv, w_g, g_bias, w_bf, w_br, w_o, *, tm):
    b, lp, _ = h.shape
    row = lambda n: pl.BlockSpec((None, tm, n), lambda bi, i: (bi, i, 0))
    return pl.pallas_call(
        _merge_kernel,
        grid=(b, lp // tm),
        in_specs=[
            row(D_MODEL), _const_spec((1, D_MODEL)), row(W_MIX), row(W_MIX),
            _const_spec((D_MODEL, 2 * D_MODEL)), _const_spec((1, 2 * D_MODEL)),
            _const_spec((W_MIX, D_MODEL)), _const_spec((W_MIX, D_MODEL)),
            _const_spec((D_MODEL, D_MODEL)),
        ],
        out_specs=row(D_MODEL),
        out_shape=jax.ShapeDtypeStruct(h.shape, F32),
        compiler_params=pltpu.CompilerParams(
            dimension_semantics=("parallel", "parallel"), vmem_limit_bytes=VMEM_LIMIT),
        name="merge",
    )(h, gain, o_fox, o_rwkv, w_g, g_bias, w_bf, w_br, w_o)


FF_CHUNK = D_FF // 2


def _ffn_kernel(h_ref, g_ref, wup_ref, wc_ref, wd_ref, out_ref, carry_ref, *, tm):
    @pl.when(pl.program_id(1) == 0)
    def _():
        carry_ref[...] = jnp.zeros_like(carry_ref)

    x = h_ref[...]
    hn = _rms_norm_rows(x, g_ref[...]).astype(BF16)
    wc = wc_ref[...]
    row = lax.broadcasted_iota(jnp.int32, (tm, FF_CHUNK), 0)

    def conv(u, col):
        prev = carry_ref[:, col:col + FF_CHUNK]
        p1 = prev[SUBLANES - 1:SUBLANES, :]
        p2 = prev[SUBLANES - 2:SUBLANES - 1, :]
        u1 = jnp.where(row == 0, p1, pltpu.roll(u, 1, 0))
        u2 = jnp.where(row == 0, p2, jnp.where(row == 1, p1, pltpu.roll(u, 2, 0)))
        carry_ref[:, col:col + FF_CHUNK] = u[tm - SUBLANES:tm, :]
        w = wc[:, col:col + FF_CHUNK]
        return w[0:1] * u2 + w[1:2] * u1 + w[2:3] * u

    acc = x
    for c in range(D_FF // FF_CHUNK):
        cg = c * FF_CHUNK
        cv = D_FF + c * FF_CHUNK
        gate = conv(_dot(hn, wup_ref[:, cg:cg + FF_CHUNK]), cg)
        val = conv(_dot(hn, wup_ref[:, cv:cv + FF_CHUNK]), cv)
        act = (gate * jax.nn.sigmoid(gate) * val).astype(BF16)
        acc = acc + _dot(act, wd_ref[cg:cg + FF_CHUNK, :])
    out_ref[...] = acc


def _ffn_call(h, gain, w_up, w_conv, w_down, *, tm):
    b, lp, _ = h.shape
    row = pl.BlockSpec((None, tm, D_MODEL), lambda bi, i: (bi, i, 0))
    return pl.pallas_call(
        functools.partial(_ffn_kernel, tm=tm),
        grid=(b, lp // tm),
        in_specs=[
            row, _const_spec((1, D_MODEL)),
            _const_spec((D_MODEL, 2 * D_FF)), _const_spec((CONV_W, 2 * D_FF)),
            _const_spec((D_FF, D_MODEL)),
        ],
        out_specs=row,
        out_shape=jax.ShapeDtypeStruct(h.shape, F32),
        scratch_shapes=[pltpu.VMEM((SUBLANES, 2 * D_FF), F32)],
        compiler_params=pltpu.CompilerParams(
            dimension_semantics=("parallel", "arbitrary"), vmem_limit_bytes=VMEM_LIMIT),
        name="conv_ffn",
    )(h, gain, w_up, w_conv, w_down)


def _row_tile(lp):
    for t in (640, 512, 384, 256, 128):
        if lp % t == 0:
            return t
    raise ValueError(f"padded length {lp} has no supported row tile")


def _pad_cols(w, n):
    return jnp.pad(w, [(0, 0)] * (w.ndim - 1) + [(0, n - w.shape[-1])])


def _prep_in_weights(w_in, f_bias, mu):
    o_f = 3 * W_MIX
    o_rw = o_f + N_HEADS
    n_rw = 3 * W_MIX + LORA_W + LORA_A + LORA_G
    o_g = o_rw + n_rw
    rw = w_in[..., o_rw:o_g]

    def rw_layout(t):
        return jnp.concatenate([
            t[..., 0:3 * W_MIX],
            _pad_cols(t[..., 3 * W_MIX:3 * W_MIX + LORA_W], LANES),
            _pad_cols(t[..., 3 * W_MIX + LORA_W:3 * W_MIX + LORA_W + LORA_A], LANES),
            t[..., 3 * W_MIX + LORA_W + LORA_A:],
        ], axis=-1)

    w_a = jnp.concatenate([
        w_in[..., 0:o_f],
        _pad_cols(w_in[..., o_f:o_rw], LANES),
        rw_layout(rw),
    ], axis=-1).astype(BF16)
    w_g = w_in[..., o_g:].astype(BF16)
    fb = _pad_cols(f_bias, LANES)[:, None, :]
    return w_a, w_g, fb, rw_layout(mu)[:, None, :]


def kernel(x, meta_tokens, norm_mix, norm_ffn, w_in, fox_q_norm, fox_k_norm, fox_f_bias,
           rwkv_shift_mu, rwkv_w0, rwkv_w_up, rwkv_a0, rwkv_a_up, rwkv_g_up, rwkv_k_k,
           rwkv_k_a, rwkv_r_k, rwkv_gn_w, rwkv_gn_b, w_branch_fox, w_branch_rwkv,
           gate_bias, w_out, ffn_up, ffn_conv, ffn_down):
    b, s, _ = x.shape
    depth = w_in.shape[0]
    length = N_META + s
    lp = -(-length // LANES) * LANES
    tm = _row_tile(lp)

    meta = jnp.broadcast_to(meta_tokens.astype(x.dtype)[None], (b, N_META, D_MODEL))
    h = jnp.concatenate([meta, x], axis=1)
    h = jnp.pad(h, ((0, 0), (0, lp - length), (0, 0)))

    w_a, w_g, f_b, mu = _prep_in_weights(w_in, fox_f_bias, rwkv_shift_mu)
    qk_gain = jnp.stack([jnp.tile(fox_q_norm, (1, N_HEADS)), jnp.tile(fox_k_norm, (1, N_HEADS))], axis=1)
    bd = (jnp.kron(jnp.eye(N_HEADS, dtype=F32), jnp.ones((HEAD_DIM, HEAD_DIM), F32)) / HEAD_DIM).astype(BF16)
    pad_rows = lambda w: jnp.pad(w, ((0, 0), (0, LANES - w.shape[1]), (0, 0))).astype(BF16)
    w_up_l, a_up_l, g_up_l = pad_rows(rwkv_w_up), pad_rows(rwkv_a_up), rwkv_g_up.astype(BF16)
    prm = jnp.stack([rwkv_w0, rwkv_a0, rwkv_k_k, rwkv_k_a, rwkv_r_k.reshape(depth, W_MIX),
                     rwkv_gn_w, rwkv_gn_b, jnp.zeros_like(rwkv_w0)], axis=1)
    w_bf, w_br, w_o = w_branch_fox.astype(BF16), w_branch_rwkv.astype(BF16), w_out.astype(BF16)
    f_up, f_down = ffn_up.astype(BF16), ffn_down.astype(BF16)
    g_mix, g_ffn, g_bias = norm_mix[:, None, :], norm_ffn[:, None, :], gate_bias[:, None, :]

    for i in range(depth):
        q, k, v, fcum, rw = _in_call(h, g_mix[i], w_a[i], qk_gain[i], bd, f_b[i], tm=tm)
        o_fox = _fox_call(q, k, v, fcum, tq=tm)
        o_rwkv = _rwkv_call(rw, mu[i], w_up_l[i], a_up_l[i], g_up_l[i], prm[i])
        h = _merge_call(h, g_mix[i], o_fox, o_rwkv, w_g[i], g_bias[i], w_bf[i], w_br[i], w_o[i], tm=tm)
        h = _ffn_call(h, g_ffn[i], f_up[i], ffn_conv[i], f_down[i], tm=tm)
    return h[:, N_META:N_META + s]
```

```python
import functools
import math

import jax
import jax.numpy as jnp
from jax import lax
from jax.experimental import pallas as pl
from jax.experimental.pallas import tpu as pltpu

D_MODEL = 1024
N_META = 16
HEAD_DIM = 64
N_HEADS = 8
W_MIX = N_HEADS * HEAD_DIM
LORA_W = 64
LORA_A = 64
LORA_G = 128
D_FF = 2816
CONV_W = 3
NORM_EPS = 1e-6
GN_EPS = HEAD_DIM * 1e-5
DEPTH = 4

LANES = 128
SUBLANES = 8
VMEM_LIMIT = 56 * 1024 * 1024

C_Q = 0
C_K = C_Q + W_MIX
C_V = C_K + W_MIX
C_F = C_V + W_MIX
C_RW = C_F + LANES
RW_COLS = 3 * W_MIX + 3 * LANES
N_A = C_RW + RW_COLS
Z_W = 3 * W_MIX
Z_A = Z_W + LANES
Z_G = Z_A + LANES

RWKV_CHUNK = 64
EXP_M05 = math.exp(-0.5)
LOG2E = math.log2(math.e)
N_PAIRS = W_MIX // LANES
FOX_TILE = LANES
FOX_L_ROWS = 16

BF16 = jnp.bfloat16
F32 = jnp.float32


def _dot(a, b):
    return jnp.dot(a, b, preferred_element_type=F32)


def _dot_nt(a, b):
    return lax.dot_general(a, b, (((1,), (1,)), ((), ())), preferred_element_type=F32)


def _rms_norm_rows(x, gain):
    ms = jnp.mean(x * x, axis=-1, keepdims=True)
    return x * lax.rsqrt(ms + NORM_EPS) * gain


def _split3_bf16(x):
    h1 = x.astype(BF16)
    r1 = x - h1.astype(F32)
    h2 = r1.astype(BF16)
    h3 = (r1 - h2.astype(F32)).astype(BF16)
    return h1, h2, h3


def _tril_ones(n, dtype):
    r = lax.broadcasted_iota(jnp.int32, (n, n), 0)
    c = lax.broadcasted_iota(jnp.int32, (n, n), 1)
    return (r >= c).astype(dtype)


def _const_spec(shape):
    return pl.BlockSpec(shape, lambda *_: (0,) * len(shape), pipeline_mode=pl.Buffered(1))


def _in_kernel(h_ref, g_ref, w_ref, wvt_ref, qkg_ref, bd_ref, fb_ref, sel_ref,
               q_ref, kx_ref, vt_ref, rw_ref, carry_ref, *, tm):
    @pl.when(pl.program_id(1) == 0)
    def _():
        carry_ref[...] = jnp.zeros_like(carry_ref)

    hn = _rms_norm_rows(h_ref[...], g_ref[...]).astype(BF16)
    bd = bd_ref[...]

    def head_norm(t, gain):
        ms = _dot((t * t).astype(BF16), bd)
        return t * lax.rsqrt(ms + NORM_EPS) * gain

    q = _dot(hn, w_ref[:, C_Q:C_Q + W_MIX])
    q_ref[...] = (head_norm(q, qkg_ref[0:1, :]) * (HEAD_DIM ** -0.5 * LOG2E)).astype(BF16)
    k = head_norm(_dot(hn, w_ref[:, C_K:C_K + W_MIX]), qkg_ref[1:2, :]).astype(BF16)
    for p in range(N_PAIRS):
        kx_ref[:, 2 * p * LANES:(2 * p + 1) * LANES] = k[:, p * LANES:(p + 1) * LANES]
    vt_ref[...] = _dot_nt(wvt_ref[...], hn).astype(BF16)
    rw_ref[...] = _dot(hn, w_ref[:, C_RW:C_RW + RW_COLS])

    f = _dot(hn, w_ref[:, C_F:C_F + LANES]) + fb_ref[...]
    log_f = jnp.minimum(f, 0.0) - jnp.log1p(jnp.exp(-jnp.abs(f)))
    tri = _tril_ones(LANES, BF16)
    carry = carry_ref[0:1, :]
    for blk in range(tm // LANES):
        rows = slice(blk * LANES, (blk + 1) * LANES)
        h1, h2, h3 = _split3_bf16(log_f[rows])
        c = _dot(tri, h1) + _dot(tri, h2) + _dot(tri, h3) + carry
        b1, b2, b3 = _split3_bf16(c * (-LOG2E))
        bias = (_dot(b1, sel_ref[0]) + _dot(b2, sel_ref[1]) + _dot(b3, sel_ref[2])).astype(BF16)
        for p in range(N_PAIRS):
            kx_ref[rows, (2 * p + 1) * LANES:(2 * p + 2) * LANES] = bias[:, p * LANES:(p + 1) * LANES]
        carry = c[LANES - 1:LANES, :]
    carry_ref[...] = jnp.broadcast_to(carry, carry_ref.shape)


def _gate_selectors():
    sel = [[[0.0] * W_MIX for _ in range(LANES)] for _ in range(3)]
    for h in range(N_HEADS):
        for term in range(3):
            sel[term][h][(h // 2) * LANES + 3 * (h % 2) + term] = 1.0
    return jnp.asarray(sel, dtype=BF16)


def _in_call(h, gain, w_a, w_vt, qk_gain, bd, f_bias, *, tm):
    b, lp, _ = h.shape
    grid = (b, lp // tm)
    row = lambda n: pl.BlockSpec((None, tm, n), lambda bi, i: (bi, i, 0))
    return pl.pallas_call(
        functools.partial(_in_kernel, tm=tm),
        grid=grid,
        in_specs=[
            row(D_MODEL),
            _const_spec((1, D_MODEL)),
            _const_spec((D_MODEL, N_A)),
            _const_spec((W_MIX, D_MODEL)),
            _const_spec((2, W_MIX)),
            _const_spec((W_MIX, W_MIX)),
            _const_spec((1, LANES)),
            _const_spec((3, LANES, W_MIX)),
        ],
        out_specs=[
            row(W_MIX), row(2 * W_MIX),
            pl.BlockSpec((None, W_MIX, tm), lambda bi, i: (bi, 0, i)),
            row(RW_COLS),
        ],
        out_shape=[
            jax.ShapeDtypeStruct((b, lp, W_MIX), BF16),
            jax.ShapeDtypeStruct((b, lp, 2 * W_MIX), BF16),
            jax.ShapeDtypeStruct((b, W_MIX, lp), BF16),
            jax.ShapeDtypeStruct((b, lp, RW_COLS), F32),
        ],
        scratch_shapes=[pltpu.VMEM((SUBLANES, LANES), F32)],
        compiler_params=pltpu.CompilerParams(
            dimension_semantics=("parallel", "arbitrary"), vmem_limit_bytes=VMEM_LIMIT),
        name="in_proj",
    )(h, gain, w_a, w_vt, qk_gain, bd, f_bias, _gate_selectors())


def _fox_kernel(q_ref, kx_ref, vt_ref, o_ref, qx_ref, m_ref, acc_ref, *, tq):
    i = pl.program_id(2)
    nt = tq // FOX_TILE
    q_t = q_ref[...].astype(F32).T
    row = lax.broadcasted_iota(jnp.int32, (LANES, FOX_TILE), 0)
    first = row < HEAD_DIM
    gate_a = jnp.where(row < 3, 1.0, 0.0).astype(BF16)
    gate_b = jnp.where(jnp.logical_and(row >= 3, row < 6), 1.0, 0.0).astype(BF16)
    for t in range(nt):
        qt = q_t[:, t * FOX_TILE:(t + 1) * FOX_TILE]
        c0 = 2 * t * FOX_TILE
        qx_ref[0:LANES, c0:c0 + FOX_TILE] = jnp.where(first, qt, 0.0).astype(BF16)
        qx_ref[LANES:2 * LANES, c0:c0 + FOX_TILE] = gate_a
        qx_ref[0:LANES, c0 + FOX_TILE:c0 + 2 * FOX_TILE] = jnp.where(first, 0.0, qt).astype(BF16)
        qx_ref[LANES:2 * LANES, c0 + FOX_TILE:c0 + 2 * FOX_TILE] = gate_b
    m_ref[...] = jnp.full_like(m_ref, -jnp.inf)
    acc_ref[...] = jnp.zeros_like(acc_ref)

    def scores(t, kx):
        return _dot(kx, qx_ref[:, 2 * t * FOX_TILE:2 * (t + 1) * FOX_TILE])

    def update(tiles, ss, vt):
        m_prev = [m_ref[t] for t in tiles]
        m_new = [jnp.maximum(mp, jnp.max(s, axis=0, keepdims=True)) for mp, s in zip(m_prev, ss)]
        alpha = [jnp.exp2(mp - mn) for mp, mn in zip(m_prev, m_new)]
        ps = [jnp.exp2((s - mn).astype(BF16)) for s, mn in zip(ss, m_new)]
        vt_ones = jnp.concatenate([vt, jnp.ones((FOX_L_ROWS, vt.shape[1]), BF16)], axis=0)
        pv = [_dot(vt_ones, p) for p in ps]
        for n, t in enumerate(tiles):
            m_ref[t] = m_new[n]
            acc_ref[t] = alpha[n] * acc_ref[t] + pv[n]

    key_i = lax.broadcasted_iota(jnp.int32, (FOX_TILE, 2 * FOX_TILE), 0)
    qry_i = lax.broadcasted_iota(jnp.int32, (FOX_TILE, 2 * FOX_TILE), 1) & (FOX_TILE - 1)
    causal = key_i <= qry_i

    def run(items):
        def issue(n):
            start, size, t0, _ = items[n]
            kx = kx_ref[pl.ds(start, size), :]
            return [scores(t, kx) for t in range(t0, nt)]

        pending = issue(0)
        for n, (start, size, t0, masked) in enumerate(items):
            upcoming = issue(n + 1) if n + 1 < len(items) else None
            vt = vt_ref[:, pl.ds(start, size)]
            tiles = list(range(t0, nt))
            update(tiles, [jnp.where(causal, s, -jnp.inf) if t == masked else s
                           for t, s in zip(tiles, pending)], vt)
            pending = upcoming

    def block_items(j):
        base = pl.multiple_of(j * tq, LANES)
        return [(base + off, size, 0, None) for off, size in _key_chunks(tq)]

    def pair_body(jj, carry):
        run(block_items(2 * jj) + block_items(2 * jj + 1))
        return carry

    lax.fori_loop(0, lax.shift_right_logical(i, 1), pair_body, 0)

    @pl.when((i & 1) == 1)
    def _():
        run(block_items(i - 1))

    base = pl.multiple_of(i * tq, LANES)
    run([(base + c * FOX_TILE, FOX_TILE, c, c) for c in range(nt)])

    dim_i = lax.broadcasted_iota(jnp.int32, (LANES, FOX_TILE), 0)
    for t in range(nt):
        o = acc_ref[t, 0:LANES, :] / acc_ref[t, LANES:LANES + 1, :]
        own = jnp.where(dim_i < HEAD_DIM, o[:, 0:FOX_TILE], o[:, FOX_TILE:2 * FOX_TILE])
        o_ref[t * FOX_TILE:(t + 1) * FOX_TILE, :] = own.T.astype(BF16)


def _key_chunks(tq):
    chunks, off = [], 0
    while off < tq:
        size = min(2 * LANES, tq - off)
        chunks.append((off, size))
        off += size
    return chunks


def _fox_call(q, kx, vt, *, tq):
    b, lp, _ = q.shape
    nt = tq // FOX_TILE
    return pl.pallas_call(
        functools.partial(_fox_kernel, tq=tq),
        grid=(b, N_PAIRS, lp // tq),
        in_specs=[
            pl.BlockSpec((None, tq, LANES), lambda bi, hp, i: (bi, i, hp)),
            pl.BlockSpec((None, lp, 2 * LANES), lambda bi, hp, i: (bi, 0, hp)),
            pl.BlockSpec((None, LANES, lp), lambda bi, hp, i: (bi, hp, 0)),
        ],
        out_specs=pl.BlockSpec((None, tq, LANES), lambda bi, hp, i: (bi, i, hp)),
        out_shape=jax.ShapeDtypeStruct((b, lp, W_MIX), BF16),
        scratch_shapes=[
            pltpu.VMEM((2 * LANES, 2 * tq), BF16),
            pltpu.VMEM((nt, 1, 2 * FOX_TILE), F32),
            pltpu.VMEM((nt, LANES + FOX_L_ROWS, 2 * FOX_TILE), F32),
        ],
        compiler_params=pltpu.CompilerParams(
            dimension_semantics=("parallel", "parallel", "arbitrary"),
            vmem_limit_bytes=VMEM_LIMIT),
        name="fox_attention",
    )(q, kx, vt)


def _rwkv_kernel(rw_ref, mu_ref, wup_ref, aup_ref, gup_ref, p_ref, o_ref, st_ref, prev_ref):
    cs = RWKV_CHUNK
    n_batch = rw_ref.shape[0]
    n_pairs = W_MIX // LANES

    @pl.when(pl.program_id(0) == 0)
    def _():
        st_ref[...] = jnp.zeros_like(st_ref)
        prev_ref[...] = jnp.zeros_like(prev_ref)

    prm = p_ref[...]
    w0, a0, k_k, k_a = prm[0:1], prm[1:2], prm[2:3], prm[3:4]
    r_k, gn_w, gn_b = prm[4:5], prm[5:6], prm[6:7]
    tri = _tril_ones(cs, BF16)

    lane = lax.broadcasted_iota(jnp.int32, (cs, LANES), 1)
    rowi = lax.broadcasted_iota(jnp.int32, (cs, LANES), 0)
    lo = lane < HEAD_DIM
    hi = jnp.logical_not(lo)
    col = lane & (HEAD_DIM - 1)
    strict = rowi > col
    incl = rowi >= col
    zeros_blk = jnp.zeros((cs, LANES), BF16)

    def half_sums(t):
        s_lo = jnp.sum(jnp.where(lo, t, 0.0), axis=-1, keepdims=True)
        s_hi = jnp.sum(jnp.where(hi, t, 0.0), axis=-1, keepdims=True)
        return jnp.where(lo, s_lo, s_hi)

    per_b = []
    for bi in range(n_batch):
        rw = rw_ref[bi]
        row = lax.broadcasted_iota(jnp.int32, rw.shape, 0)
        rw_prev = jnp.where(row == 0, prev_ref[bi, SUBLANES - 1:SUBLANES, :], pltpu.roll(rw, 1, 0))
        prev_ref[bi] = rw[cs - SUBLANES:cs, :]
        z = rw + mu_ref[...] * (rw_prev - rw)
        r_all = z[:, 0:W_MIX]
        kr = z[:, W_MIX:2 * W_MIX]
        v_all = z[:, 2 * W_MIX:3 * W_MIX]
        zw = z[:, Z_W:Z_W + LANES]
        za = z[:, Z_A:Z_A + LANES]
        zg = z[:, Z_G:Z_G + LANES]

        lw = -EXP_M05 * jax.nn.sigmoid(w0 + _dot(jnp.tanh(zw).astype(BF16), wup_ref[...]))
        a_rate = jax.nn.sigmoid(a0 + _dot(za.astype(BF16), aup_ref[...]))
        g_all = _dot(jax.nn.sigmoid(zg).astype(BF16), gup_ref[...])
        k_all = kr * (1.0 + (a_rate - 1.0) * k_a)
        kk = kr * k_k
        kk = jnp.concatenate(
            [kk[:, p * LANES:(p + 1) * LANES]
             / jnp.maximum(jnp.sqrt(half_sums(jnp.square(kk[:, p * LANES:(p + 1) * LANES]))), 1e-12)
             for p in range(n_pairs)], axis=1)
        b_all = kk * a_rate

        l1, l2, l3 = _split3_bf16(lw)
        cw = _dot(tri, l1) + _dot(tri, l2) + _dot(tri, l3)
        cw_last = cw[cs - 1:cs, :]
        c_ref = 0.5 * cw_last
        e_cw = jnp.exp(cw)
        e_cwm = jnp.exp(cw - lw)
        e_nref = jnp.exp(-c_ref)
        e_neg = jnp.exp(c_ref - cw)
        e_last = jnp.exp(cw_last - cw)
        per_b.append(dict(
            r=r_all, k=k_all, v=v_all, g=g_all,
            v_roll=pltpu.roll(v_all, HEAD_DIM, 1),
            at=-kk * (e_cwm * e_nref), rt=r_all * (e_cw * e_nref),
            bt=b_all * e_neg, kt=k_all * e_neg,
            a0=-kk * e_cwm, r0=r_all * e_cw,
            bh=b_all * e_last, kh=k_all * e_last,
            e_total=jnp.exp(cw_last)))

    heads = [(bi, h) for bi in range(n_batch) for h in range(N_HEADS)]

    def grp(bi, h, name):
        p = h // 2
        return per_b[bi][name][:, p * LANES:(p + 1) * LANES]

    def mine(h):
        return hi if h % 2 else lo

    def stack_masked(bi, h, top, bot):
        t = jnp.concatenate([grp(bi, h, top), grp(bi, h, bot)], axis=0)
        m = jnp.concatenate([mine(h), mine(h)], axis=0)
        return jnp.where(m, t, 0.0).astype(BF16)

    m_all = [_dot_nt(stack_masked(bi, h, "at", "rt"),
                     jnp.concatenate([grp(bi, h, "bt"), grp(bi, h, "kt")], axis=0).astype(BF16))
             for bi, h in heads]
    m_top = [jnp.where(strict, m[0:cs], 0.0) for m in m_all]
    m_bot = [jnp.where(incl, m[cs:2 * cs], 0.0).astype(BF16) for m in m_all]
    v_hi = [jnp.where(hi, grp(bi, h, "v" if h % 2 else "v_roll"), 0.0) for bi, h in heads]

    s_old = [st_ref[i] for i in range(len(heads))]
    s_in = [_dot_nt(stack_masked(bi, h, "a0", "r0"), s_old[i].astype(BF16))
            for i, (bi, h) in enumerate(heads)]

    w = [jnp.where(lo, m_top[i],
                   s_in[i][0:cs] + _dot(m_top[i].astype(BF16),
                                        jnp.concatenate([zeros_blk, v_hi[i].astype(BF16)], axis=0)))
         for i in range(len(heads))]
    for _ in range(int(math.log2(cs))):
        pw = [_dot(w[i][:, 0:HEAD_DIM].astype(BF16), w[i].astype(BF16)) for i in range(len(heads))]
        w = [pw[i] + jnp.where(lo, 0.0, w[i]) for i in range(len(heads))]

    z32 = [jnp.concatenate([jnp.where(lo, 0.0, w[i]), v_hi[i]], axis=0) for i in range(len(heads))]
    y = [s_in[i][cs:2 * cs] + _dot(m_bot[i], z32[i].astype(BF16)) for i in range(len(heads))]

    for i, (bi, h) in enumerate(heads):
        st_ref[i] = (s_old[i] * grp(bi, h, "e_total")
                     + _dot(z32[i].T.astype(BF16), stack_masked(bi, h, "bh", "kh")))

    for bi in range(n_batch):
        for p in range(n_pairs):
            i_even = bi * N_HEADS + 2 * p
            sl = slice(p * LANES, (p + 1) * LANES)
            yp = pltpu.roll(y[i_even], HEAD_DIM, 1) + y[i_even + 1]
            mean = half_sums(yp) * (1.0 / HEAD_DIM)
            d = yp - mean
            var = half_sums(d * d) * (1.0 / HEAD_DIM)
            yn = d * lax.rsqrt(var + GN_EPS) * gn_w[:, sl] + gn_b[:, sl]
            pb = per_b[bi]
            bonus = half_sums(pb["r"][:, sl] * pb["k"][:, sl] * r_k[:, sl]) * pb["v"][:, sl]
            o_ref[bi, :, sl] = ((yn + bonus) * pb["g"][:, sl]).astype(BF16)


def _rwkv_call(rw, mu, w_up, a_up, g_up, prm):
    b, lp, _ = rw.shape
    cs = RWKV_CHUNK
    return pl.pallas_call(
        _rwkv_kernel,
        grid=(lp // cs,),
        in_specs=[
            pl.BlockSpec((b, cs, RW_COLS), lambda c: (0, c, 0)),
            _const_spec((1, RW_COLS)),
            _const_spec((LANES, W_MIX)),
            _const_spec((LANES, W_MIX)),
            _const_spec((LANES, W_MIX)),
            _const_spec((SUBLANES, W_MIX)),
        ],
        out_specs=pl.BlockSpec((b, cs, W_MIX), lambda c: (0, c, 0)),
        out_shape=jax.ShapeDtypeStruct((b, lp, W_MIX), BF16),
        scratch_shapes=[
            pltpu.VMEM((b * N_HEADS, 2 * cs, LANES), F32),
            pltpu.VMEM((b, SUBLANES, RW_COLS), F32),
        ],
        compiler_params=pltpu.CompilerParams(
            dimension_semantics=("arbitrary",), vmem_limit_bytes=VMEM_LIMIT),
        name="rwkv7",
    )(rw, mu, w_up, a_up, g_up, prm)


def _merge_kernel(h_ref, g_ref, of_ref, or_ref, wg_ref, gb_ref, wbf_ref, wbr_ref, wo_ref, out_ref):
    x = h_ref[...]
    hn = _rms_norm_rows(x, g_ref[...]).astype(BF16)
    gates = jax.nn.sigmoid(_dot(hn, wg_ref[...]) + gb_ref[...])
    merged = (gates[:, 0:D_MODEL] * _dot(of_ref[...], wbf_ref[...])
              + gates[:, D_MODEL:2 * D_MODEL] * _dot(or_ref[...], wbr_ref[...]))
    out_ref[...] = x + _dot(merged.astype(BF16), wo_ref[...])


def _merge_call(h, gain, o_fox, o_rwkv, w_g, g_bias, w_bf, w_br, w_o, *, tm):
    b, lp, _ = h.shape
    row = lambda n: pl.BlockSpec((None, tm, n), lambda bi, i: (bi, i, 0))
    return pl.pallas_call(
        _merge_kernel,
        grid=(b, lp // tm),
        in_specs=[
            row(D_MODEL), _const_spec((1, D_MODEL)), row(W_MIX), row(W_MIX),
            _const_spec((D_MODEL, 2 * D_MODEL)), _const_spec((1, 2 * D_MODEL)),
            _const_spec((W_MIX, D_MODEL)), _const_spec((W_MIX, D_MODEL)),
            _const_spec((D_MODEL, D_MODEL)),
        ],
        out_specs=row(D_MODEL),
        out_shape=jax.ShapeDtypeStruct(h.shape, F32),
        compiler_params=pltpu.CompilerParams(
            dimension_semantics=("parallel", "parallel"), vmem_limit_bytes=VMEM_LIMIT),
        name="merge",
    )(h, gain, o_fox, o_rwkv, w_g, g_bias, w_bf, w_br, w_o)


FF_CHUNK = D_FF // 2


def _ffn_kernel(h_ref, g_ref, wup_ref, wc_ref, wd_ref, out_ref, carry_ref, *, tm):
    @pl.when(pl.program_id(1) == 0)
    def _():
        carry_ref[...] = jnp.zeros_like(carry_ref)

    x = h_ref[...]
    hn = _rms_norm_rows(x, g_ref[...]).astype(BF16)
    wc = wc_ref[...]
    row = lax.broadcasted_iota(jnp.int32, (tm, FF_CHUNK), 0)

    def conv(u, col):
        prev = carry_ref[:, col:col + FF_CHUNK]
        p1 = prev[SUBLANES - 1:SUBLANES, :]
        p2 = prev[SUBLANES - 2:SUBLANES - 1, :]
        u1 = jnp.where(row == 0, p1, pltpu.roll(u, 1, 0))
        u2 = jnp.where(row == 0, p2, jnp.where(row == 1, p1, pltpu.roll(u, 2, 0)))
        carry_ref[:, col:col + FF_CHUNK] = u[tm - SUBLANES:tm, :]
        w = wc[:, col:col + FF_CHUNK]
        return w[0:1] * u2 + w[1:2] * u1 + w[2:3] * u

    acc = x
    for c in range(D_FF // FF_CHUNK):
        cg = c * FF_CHUNK
        cv = D_FF + c * FF_CHUNK
        gate = conv(_dot(hn, wup_ref[:, cg:cg + FF_CHUNK]), cg)
        val = conv(_dot(hn, wup_ref[:, cv:cv + FF_CHUNK]), cv)
        act = (gate * jax.nn.sigmoid(gate) * val).astype(BF16)
        acc = acc + _dot(act, wd_ref[cg:cg + FF_CHUNK, :])
    out_ref[...] = acc


def _ffn_call(h, gain, w_up, w_conv, w_down, *, tm):
    b, lp, _ = h.shape
    row = pl.BlockSpec((None, tm, D_MODEL), lambda bi, i: (bi, i, 0))
    return pl.pallas_call(
        functools.partial(_ffn_kernel, tm=tm),
        grid=(b, lp // tm),
        in_specs=[
            row, _const_spec((1, D_MODEL)),
            _const_spec((D_MODEL, 2 * D_FF)), _const_spec((CONV_W, 2 * D_FF)),
            _const_spec((D_FF, D_MODEL)),
        ],
        out_specs=row,
        out_shape=jax.ShapeDtypeStruct(h.shape, F32),
        scratch_shapes=[pltpu.VMEM((SUBLANES, 2 * D_FF), F32)],
        compiler_params=pltpu.CompilerParams(
            dimension_semantics=("parallel", "arbitrary"), vmem_limit_bytes=VMEM_LIMIT),
        name="conv_ffn",
    )(h, gain, w_up, w_conv, w_down)


def _row_tile(lp):
    for t in (640, 512, 384, 256, 128):
        if lp % t == 0:
            return t
    raise ValueError(f"padded length {lp} has no supported row tile")


def _pad_cols(w, n):
    return jnp.pad(w, [(0, 0)] * (w.ndim - 1) + [(0, n - w.shape[-1])])


def _prep_in_weights(w_in, f_bias, mu):
    o_f = 3 * W_MIX
    o_rw = o_f + N_HEADS
    n_rw = 3 * W_MIX + LORA_W + LORA_A + LORA_G
    o_g = o_rw + n_rw
    rw = w_in[..., o_rw:o_g]

    def rw_layout(t):
        return jnp.concatenate([
            t[..., 0:3 * W_MIX],
            _pad_cols(t[..., 3 * W_MIX:3 * W_MIX + LORA_W], LANES),
            _pad_cols(t[..., 3 * W_MIX + LORA_W:3 * W_MIX + LORA_W + LORA_A], LANES),
            t[..., 3 * W_MIX + LORA_W + LORA_A:],
        ], axis=-1)

    w_a = jnp.concatenate([
        w_in[..., 0:o_f],
        _pad_cols(w_in[..., o_f:o_rw], LANES),
        rw_layout(rw),
    ], axis=-1).astype(BF16)
    w_vt = jnp.swapaxes(w_in[..., 2 * W_MIX:3 * W_MIX], -1, -2).astype(BF16)
    w_g = w_in[..., o_g:].astype(BF16)
    fb = _pad_cols(f_bias, LANES)[:, None, :]
    return w_a, w_vt, w_g, fb, rw_layout(mu)[:, None, :]


def kernel(x, meta_tokens, norm_mix, norm_ffn, w_in, fox_q_norm, fox_k_norm, fox_f_bias,
           rwkv_shift_mu, rwkv_w0, rwkv_w_up, rwkv_a0, rwkv_a_up, rwkv_g_up, rwkv_k_k,
           rwkv_k_a, rwkv_r_k, rwkv_gn_w, rwkv_gn_b, w_branch_fox, w_branch_rwkv,
           gate_bias, w_out, ffn_up, ffn_conv, ffn_down):
    b, s, _ = x.shape
    depth = w_in.shape[0]
    length = N_META + s
    lp = -(-length // LANES) * LANES
    tm = _row_tile(lp)

    meta = jnp.broadcast_to(meta_tokens.astype(x.dtype)[None], (b, N_META, D_MODEL))
    h = jnp.concatenate([meta, x], axis=1)
    h = jnp.pad(h, ((0, 0), (0, lp - length), (0, 0)))

    w_a, w_vt, w_g, f_b, mu = _prep_in_weights(w_in, fox_f_bias, rwkv_shift_mu)
    qk_gain = jnp.stack([jnp.tile(fox_q_norm, (1, N_HEADS)), jnp.tile(fox_k_norm, (1, N_HEADS))], axis=1)
    bd = (jnp.kron(jnp.eye(N_HEADS, dtype=F32), jnp.ones((HEAD_DIM, HEAD_DIM), F32)) / HEAD_DIM).astype(BF16)
    pad_rows = lambda w: jnp.pad(w, ((0, 0), (0, LANES - w.shape[1]), (0, 0))).astype(BF16)
    w_up_l, a_up_l, g_up_l = pad_rows(rwkv_w_up), pad_rows(rwkv_a_up), rwkv_g_up.astype(BF16)
    prm = jnp.stack([rwkv_w0, rwkv_a0, rwkv_k_k, rwkv_k_a, rwkv_r_k.reshape(depth, W_MIX),
                     rwkv_gn_w, rwkv_gn_b, jnp.zeros_like(rwkv_w0)], axis=1)
    w_bf, w_br, w_o = w_branch_fox.astype(BF16), w_branch_rwkv.astype(BF16), w_out.astype(BF16)
    f_up, f_down = ffn_up.astype(BF16), ffn_down.astype(BF16)
    g_mix, g_ffn, g_bias = norm_mix[:, None, :], norm_ffn[:, None, :], gate_bias[:, None, :]

    for i in range(depth):
        q, kx, vt, rw = _in_call(h, g_mix[i], w_a[i], w_vt[i], qk_gain[i], bd, f_b[i], tm=tm)
        o_fox = _fox_call(q, kx, vt, tq=tm)
        o_rwkv = _rwkv_call(rw, mu[i], w_up_l[i], a_up_l[i], g_up_l[i], prm[i])
        h = _merge_call(h, g_mix[i], o_fox, o_rwkv, w_g[i], g_bias[i], w_bf[i], w_br[i], w_o[i], tm=tm)
        h = _ffn_call(h, g_ffn[i], f_up[i], ffn_conv[i], f_down[i], tm=tm)
    return h[:, N_META:N_META + s]
```

```python
import functools
import math

import jax
import jax.numpy as jnp
from jax import lax
from jax.experimental import pallas as pl
from jax.experimental.pallas import tpu as pltpu

D_MODEL = 1024
N_META = 16
HEAD_DIM = 64
N_HEADS = 8
W_MIX = N_HEADS * HEAD_DIM
LORA_W = 64
LORA_A = 64
LORA_G = 128
D_FF = 2816
CONV_W = 3
NORM_EPS = 1e-6
GN_EPS = HEAD_DIM * 1e-5
DEPTH = 4

LANES = 128
SUBLANES = 8
VMEM_LIMIT = 56 * 1024 * 1024

C_Q = 0
C_K = C_Q + W_MIX
C_V = C_K + W_MIX
C_F = C_V + W_MIX
C_RW = C_F + LANES
RW_COLS = 3 * W_MIX + 3 * LANES
N_A = C_RW + RW_COLS
Z_W = 3 * W_MIX
Z_A = Z_W + LANES
Z_G = Z_A + LANES

RWKV_CHUNK = 64
RWKV_CHUNKS_PER_STEP = (5, 4, 2, 1)
EXP_M05 = math.exp(-0.5)
LOG2E = math.log2(math.e)
N_PAIRS = W_MIX // LANES
FOX_TILE = LANES
FOX_L_ROWS = 16

BF16 = jnp.bfloat16
F32 = jnp.float32


def _dot(a, b):
    return jnp.dot(a, b, preferred_element_type=F32)


def _dot_nt(a, b):
    return lax.dot_general(a, b, (((1,), (1,)), ((), ())), preferred_element_type=F32)


def _rms_norm_rows(x, gain):
    ms = jnp.mean(x * x, axis=-1, keepdims=True)
    return x * lax.rsqrt(ms + NORM_EPS) * gain


def _split3_bf16(x):
    h1 = x.astype(BF16)
    r1 = x - h1.astype(F32)
    h2 = r1.astype(BF16)
    h3 = (r1 - h2.astype(F32)).astype(BF16)
    return h1, h2, h3


def _tril_ones(n, dtype):
    r = lax.broadcasted_iota(jnp.int32, (n, n), 0)
    c = lax.broadcasted_iota(jnp.int32, (n, n), 1)
    return (r >= c).astype(dtype)


def _const_spec(shape):
    return pl.BlockSpec(shape, lambda *_: (0,) * len(shape), pipeline_mode=pl.Buffered(1))


def _in_kernel(h_ref, g_ref, w_ref, wvt_ref, qkg_ref, bd_ref, fb_ref, sel_ref,
               q_ref, kx_ref, vt_ref, rw_ref, carry_ref, *, tm):
    @pl.when(pl.program_id(1) == 0)
    def _():
        carry_ref[...] = jnp.zeros_like(carry_ref)

    hn = _rms_norm_rows(h_ref[...], g_ref[...]).astype(BF16)
    bd = bd_ref[...]

    def head_norm(t, gain):
        ms = _dot((t * t).astype(BF16), bd)
        return t * lax.rsqrt(ms + NORM_EPS) * gain

    q = _dot(hn, w_ref[:, C_Q:C_Q + W_MIX])
    q_ref[...] = (head_norm(q, qkg_ref[0:1, :]) * (HEAD_DIM ** -0.5 * LOG2E)).astype(BF16)
    k = head_norm(_dot(hn, w_ref[:, C_K:C_K + W_MIX]), qkg_ref[1:2, :]).astype(BF16)
    for p in range(N_PAIRS):
        kx_ref[:, 2 * p * LANES:(2 * p + 1) * LANES] = k[:, p * LANES:(p + 1) * LANES]
    vt_ref[...] = _dot_nt(wvt_ref[...], hn).astype(BF16)
    rw_ref[...] = _dot(hn, w_ref[:, C_RW:C_RW + RW_COLS])

    f = _dot(hn, w_ref[:, C_F:C_F + LANES]) + fb_ref[...]
    log_f = jnp.minimum(f, 0.0) - jnp.log1p(jnp.exp(-jnp.abs(f)))
    tri = _tril_ones(LANES, BF16)
    carry = carry_ref[0:1, :]
    for blk in range(tm // LANES):
        rows = slice(blk * LANES, (blk + 1) * LANES)
        h1, h2, h3 = _split3_bf16(log_f[rows])
        c = _dot(tri, h1) + _dot(tri, h2) + _dot(tri, h3) + carry
        b1, b2, b3 = _split3_bf16(c * (-LOG2E))
        bias = (_dot(b1, sel_ref[0]) + _dot(b2, sel_ref[1]) + _dot(b3, sel_ref[2])).astype(BF16)
        for p in range(N_PAIRS):
            kx_ref[rows, (2 * p + 1) * LANES:(2 * p + 2) * LANES] = bias[:, p * LANES:(p + 1) * LANES]
        carry = c[LANES - 1:LANES, :]
    carry_ref[...] = jnp.broadcast_to(carry, carry_ref.shape)


def _gate_selectors():
    sel = [[[0.0] * W_MIX for _ in range(LANES)] for _ in range(3)]
    for h in range(N_HEADS):
        for term in range(3):
            sel[term][h][(h // 2) * LANES + 3 * (h % 2) + term] = 1.0
    return jnp.asarray(sel, dtype=BF16)


def _in_call(h, gain, w_a, w_vt, qk_gain, bd, f_bias, *, tm):
    b, lp, _ = h.shape
    grid = (b, lp // tm)
    row = lambda n: pl.BlockSpec((None, tm, n), lambda bi, i: (bi, i, 0))
    return pl.pallas_call(
        functools.partial(_in_kernel, tm=tm),
        grid=grid,
        in_specs=[
            row(D_MODEL),
            _const_spec((1, D_MODEL)),
            _const_spec((D_MODEL, N_A)),
            _const_spec((W_MIX, D_MODEL)),
            _const_spec((2, W_MIX)),
            _const_spec((W_MIX, W_MIX)),
            _const_spec((1, LANES)),
            _const_spec((3, LANES, W_MIX)),
        ],
        out_specs=[
            row(W_MIX), row(2 * W_MIX),
            pl.BlockSpec((None, W_MIX, tm), lambda bi, i: (bi, 0, i)),
            row(RW_COLS),
        ],
        out_shape=[
            jax.ShapeDtypeStruct((b, lp, W_MIX), BF16),
            jax.ShapeDtypeStruct((b, lp, 2 * W_MIX), BF16),
            jax.ShapeDtypeStruct((b, W_MIX, lp), BF16),
            jax.ShapeDtypeStruct((b, lp, RW_COLS), F32),
        ],
        scratch_shapes=[pltpu.VMEM((SUBLANES, LANES), F32)],
        compiler_params=pltpu.CompilerParams(
            dimension_semantics=("parallel", "arbitrary"), vmem_limit_bytes=VMEM_LIMIT),
        name="in_proj",
    )(h, gain, w_a, w_vt, qk_gain, bd, f_bias, _gate_selectors())


def _fox_kernel(q_ref, kx_ref, vt_ref, o_ref, qx_ref, m_ref, acc_ref, *, tq):
    i = pl.program_id(2)
    nt = tq // FOX_TILE
    q_t = q_ref[...].astype(F32).T
    row = lax.broadcasted_iota(jnp.int32, (LANES, FOX_TILE), 0)
    first = row < HEAD_DIM
    gate_a = jnp.where(row < 3, 1.0, 0.0).astype(BF16)
    gate_b = jnp.where(jnp.logical_and(row >= 3, row < 6), 1.0, 0.0).astype(BF16)
    for t in range(nt):
        qt = q_t[:, t * FOX_TILE:(t + 1) * FOX_TILE]
        c0 = 2 * t * FOX_TILE
        qx_ref[0:LANES, c0:c0 + FOX_TILE] = jnp.where(first, qt, 0.0).astype(BF16)
        qx_ref[LANES:2 * LANES, c0:c0 + FOX_TILE] = gate_a
        qx_ref[0:LANES, c0 + FOX_TILE:c0 + 2 * FOX_TILE] = jnp.where(first, 0.0, qt).astype(BF16)
        qx_ref[LANES:2 * LANES, c0 + FOX_TILE:c0 + 2 * FOX_TILE] = gate_b
    m_ref[...] = jnp.full_like(m_ref, -jnp.inf)
    acc_ref[...] = jnp.zeros_like(acc_ref)

    def scores(t, kx):
        return _dot(kx, qx_ref[:, 2 * t * FOX_TILE:2 * (t + 1) * FOX_TILE])

    def update(tiles, ss, vt):
        m_prev = [m_ref[t] for t in tiles]
        m_new = [jnp.maximum(mp, jnp.max(s, axis=0, keepdims=True)) for mp, s in zip(m_prev, ss)]
        alpha = [jnp.exp2(mp - mn) for mp, mn in zip(m_prev, m_new)]
        ps = [jnp.exp2((s - mn).astype(BF16)) for s, mn in zip(ss, m_new)]
        vt_ones = jnp.concatenate([vt, jnp.ones((FOX_L_ROWS, vt.shape[1]), BF16)], axis=0)
        pv = [_dot(vt_ones, p) for p in ps]
        for n, t in enumerate(tiles):
            m_ref[t] = m_new[n]
            acc_ref[t] = alpha[n] * acc_ref[t] + pv[n]

    def causal_mask(s, rel, t):
        if rel is None or rel + s.shape[0] - 1 <= t * FOX_TILE:
            return s
        key_i = lax.broadcasted_iota(jnp.int32, s.shape, 0) + rel
        qry_i = (lax.broadcasted_iota(jnp.int32, s.shape, 1) & (FOX_TILE - 1)) + t * FOX_TILE
        return jnp.where(key_i <= qry_i, s, -jnp.inf)

    def run(items):
        def issue(n):
            start, size, t0, _ = items[n]
            kx = kx_ref[pl.ds(start, size), :]
            return [scores(t, kx) for t in range(t0, nt)]

        pending = issue(0)
        for n, (start, size, t0, rel) in enumerate(items):
            upcoming = issue(n + 1) if n + 1 < len(items) else None
            vt = vt_ref[:, pl.ds(start, size)]
            tiles = list(range(t0, nt))
            update(tiles, [causal_mask(s, rel, t) for t, s in zip(tiles, pending)], vt)
            pending = upcoming

    def span_items(j, n_blocks):
        base = pl.multiple_of(j * tq, LANES)
        return [(base + off, size, 0, None) for off, size in _key_chunks(n_blocks * tq)]

    def pair_body(jj, carry):
        run(span_items(2 * jj, 2))
        return carry

    lax.fori_loop(0, lax.shift_right_logical(i, 1), pair_body, 0)

    @pl.when((i & 1) == 1)
    def _():
        run(span_items(i - 1, 1))

    base = pl.multiple_of(i * tq, LANES)
    run([(base + off, size, off // FOX_TILE, off) for off, size in _key_chunks(tq)])

    dim_i = lax.broadcasted_iota(jnp.int32, (LANES, FOX_TILE), 0)
    for t in range(nt):
        o = acc_ref[t, 0:LANES, :] / acc_ref[t, LANES:LANES + 1, :]
        own = jnp.where(dim_i < HEAD_DIM, o[:, 0:FOX_TILE], o[:, FOX_TILE:2 * FOX_TILE])
        o_ref[t * FOX_TILE:(t + 1) * FOX_TILE, :] = own.T.astype(BF16)


def _key_chunks(tq):
    chunks, off = [], 0
    while off < tq:
        size = min(2 * LANES, tq - off)
        chunks.append((off, size))
        off += size
    return chunks


def _fox_call(q, kx, vt, *, tq):
    b, lp, _ = q.shape
    nt = tq // FOX_TILE
    return pl.pallas_call(
        functools.partial(_fox_kernel, tq=tq),
        grid=(b, N_PAIRS, lp // tq),
        in_specs=[
            pl.BlockSpec((None, tq, LANES), lambda bi, hp, i: (bi, i, hp)),
            pl.BlockSpec((None, lp, 2 * LANES), lambda bi, hp, i: (bi, 0, hp)),
            pl.BlockSpec((None, LANES, lp), lambda bi, hp, i: (bi, hp, 0)),
        ],
        out_specs=pl.BlockSpec((None, tq, LANES), lambda bi, hp, i: (bi, i, hp)),
        out_shape=jax.ShapeDtypeStruct((b, lp, W_MIX), BF16),
        scratch_shapes=[
            pltpu.VMEM((2 * LANES, 2 * tq), BF16),
            pltpu.VMEM((nt, 1, 2 * FOX_TILE), F32),
            pltpu.VMEM((nt, LANES + FOX_L_ROWS, 2 * FOX_TILE), F32),
        ],
        compiler_params=pltpu.CompilerParams(
            dimension_semantics=("parallel", "parallel", "arbitrary"),
            vmem_limit_bytes=VMEM_LIMIT),
        name="fox_attention",
    )(q, kx, vt)


def _rwkv_kernel(rw_ref, mu_ref, wup_ref, aup_ref, gup_ref, p_ref, o_ref, st_ref, prev_ref):
    cs = RWKV_CHUNK
    n_batch, rows, _ = rw_ref.shape
    n_chunks = rows // cs
    n_levels = int(math.log2(cs))

    @pl.when(pl.program_id(0) == 0)
    def _():
        st_ref[...] = jnp.zeros_like(st_ref)
        prev_ref[...] = jnp.zeros_like(prev_ref)

    prm = p_ref[...]
    w0, a0, k_k, k_a = prm[0:1], prm[1:2], prm[2:3], prm[3:4]
    r_k, gn_w, gn_b = prm[4:5], prm[5:6], prm[6:7]
    tri = _tril_ones(cs, BF16)

    lane = lax.broadcasted_iota(jnp.int32, (cs, LANES), 1)
    rowi = lax.broadcasted_iota(jnp.int32, (cs, LANES), 0)
    lo = lane < HEAD_DIM
    hi = jnp.logical_not(lo)
    col = lane & (HEAD_DIM - 1)
    strict = rowi > col
    incl = rowi >= col
    eye_hi = jnp.where(jnp.logical_and(hi, rowi == col), 1.0, 0.0)
    zeros_blk = jnp.zeros((cs, LANES), BF16)
    row_full = lax.broadcasted_iota(jnp.int32, (cs, RW_COLS), 0)

    def half_sums(t):
        s_lo = jnp.sum(jnp.where(lo, t, 0.0), axis=-1, keepdims=True)
        s_hi = jnp.sum(jnp.where(hi, t, 0.0), axis=-1, keepdims=True)
        return jnp.where(lo, s_lo, s_hi)

    heads = [(bi, h) for bi in range(n_batch) for h in range(N_HEADS)]
    n_heads = len(heads)
    pre = {}

    def shift_and_maps(c):
        for bi in range(n_batch):
            rw = rw_ref[bi, c * cs:(c + 1) * cs, :]
            before = (prev_ref[bi, SUBLANES - 1:SUBLANES, :] if c == 0
                      else rw_ref[bi, c * cs - 1:c * cs, :])
            rw_prev = jnp.where(row_full == 0, before, pltpu.roll(rw, 1, 0))
            z = rw + mu_ref[...] * (rw_prev - rw)
            r_all = z[:, 0:W_MIX]
            kr = z[:, W_MIX:2 * W_MIX]
            v_all = z[:, 2 * W_MIX:3 * W_MIX]
            zw = z[:, Z_W:Z_W + LANES]
            za = z[:, Z_A:Z_A + LANES]
            zg = z[:, Z_G:Z_G + LANES]
            lw = -EXP_M05 * jax.nn.sigmoid(w0 + _dot(jnp.tanh(zw).astype(BF16), wup_ref[...]))
            a_rate = jax.nn.sigmoid(a0 + _dot(za.astype(BF16), aup_ref[...]))
            g_all = _dot(jax.nn.sigmoid(zg).astype(BF16), gup_ref[...])
            k_all = kr * (1.0 + (a_rate - 1.0) * k_a)
            kk = kr * k_k
            kk = jnp.concatenate(
                [kk[:, p * LANES:(p + 1) * LANES]
                 / jnp.maximum(jnp.sqrt(half_sums(jnp.square(kk[:, p * LANES:(p + 1) * LANES]))), 1e-12)
                 for p in range(N_PAIRS)], axis=1)
            pre[c, bi] = dict(r=r_all, k=k_all, v=v_all, g=g_all, lw=lw, kk=kk, b=kk * a_rate)

    def decay_operands(c):
        for bi in range(n_batch):
            d = pre[c, bi]
            lw, kk, b_all, r_all, k_all, v_all = d["lw"], d["kk"], d["b"], d["r"], d["k"], d["v"]
            l1, l2, l3 = _split3_bf16(lw)
            cw = _dot(tri, l1) + _dot(tri, l2) + _dot(tri, l3)
            cw_last = cw[cs - 1:cs, :]
            c_ref = 0.5 * cw_last
            e_cw = jnp.exp(cw)
            e_cwm = jnp.exp(cw - lw)
            e_nref = jnp.exp(-c_ref)
            e_neg = jnp.exp(c_ref - cw)
            e_last = jnp.exp(cw_last - cw)
            d.update(
                v_roll=pltpu.roll(v_all, HEAD_DIM, 1),
                at=-kk * (e_cwm * e_nref), rt=r_all * (e_cw * e_nref),
                bt=b_all * e_neg, kt=k_all * e_neg,
                a0=-kk * e_cwm, r0=r_all * e_cw,
                bh=b_all * e_last, kh=k_all * e_last,
                e_total=jnp.exp(cw_last))

    def grp(c, bi, h, name):
        p = h // 2
        return pre[c, bi][name][:, p * LANES:(p + 1) * LANES]

    def mine(h):
        return hi if h % 2 else lo

    def stack_masked(c, bi, h, top, bot):
        t = jnp.concatenate([grp(c, bi, h, top), grp(c, bi, h, bot)], axis=0)
        m = jnp.concatenate([mine(h), mine(h)], axis=0)
        return jnp.where(m, t, 0.0).astype(BF16)

    m_bot, v_hi, x_v, w = {}, {}, {}, {}

    def score_matrices(c):
        m_all = [_dot_nt(stack_masked(c, bi, h, "at", "rt"),
                         jnp.concatenate([grp(c, bi, h, "bt"), grp(c, bi, h, "kt")], axis=0).astype(BF16))
                 for bi, h in heads]
        m_top = [jnp.where(strict, m[0:cs], 0.0) for m in m_all]
        m_bot[c] = [jnp.where(incl, m[cs:2 * cs], 0.0).astype(BF16) for m in m_all]
        v_hi[c] = [jnp.where(hi, grp(c, bi, h, "v" if h % 2 else "v_roll"), 0.0) for bi, h in heads]
        x_v[c] = [_dot(m_top[i].astype(BF16), jnp.concatenate([zeros_blk, v_hi[c][i].astype(BF16)], axis=0))
                  for i in range(n_heads)]
        w[c] = [jnp.where(lo, m_top[i], eye_hi) for i in range(n_heads)]

    def t_levels(c, count):
        for _ in range(count):
            pw = [_dot(w[c][i][:, 0:HEAD_DIM].astype(BF16), w[c][i].astype(BF16)) for i in range(n_heads)]
            w[c] = [pw[i] + jnp.where(lo, 0.0, w[c][i]) for i in range(n_heads)]

    state = [st_ref[i] for i in range(n_heads)]
    s_in, u_hi, y = {}, {}, {}

    def chain_state_in(c):
        s_in[c] = [_dot_nt(stack_masked(c, bi, h, "a0", "r0"), state[i].astype(BF16))
                   for i, (bi, h) in enumerate(heads)]

    def chain_solve(c):
        t_hi = [jnp.where(lo, 0.0, w[c][i]).astype(BF16) for i in range(n_heads)]
        u_hi[c] = [_dot(t_hi[i], jnp.concatenate([zeros_blk, (s_in[c][i][0:cs] + x_v[c][i]).astype(BF16)], axis=0))
                   for i in range(n_heads)]

    def chain_out(c):
        z32 = [jnp.concatenate([u_hi[c][i], v_hi[c][i]], axis=0) for i in range(n_heads)]
        for i, (bi, h) in enumerate(heads):
            state[i] = (state[i] * grp(c, bi, h, "e_total")
                        + _dot(z32[i].T.astype(BF16), stack_masked(c, bi, h, "bh", "kh")))
        y[c] = [s_in[c][i][cs:2 * cs] + _dot(m_bot[c][i], z32[i].astype(BF16)) for i in range(n_heads)]

    def finish(c):
        for bi in range(n_batch):
            d = pre[c, bi]
            for p in range(N_PAIRS):
                i_even = bi * N_HEADS + 2 * p
                sl = slice(p * LANES, (p + 1) * LANES)
                yp = pltpu.roll(y[c][i_even], HEAD_DIM, 1) + y[c][i_even + 1]
                mean = half_sums(yp) * (1.0 / HEAD_DIM)
                dv = yp - mean
                var = half_sums(dv * dv) * (1.0 / HEAD_DIM)
                yn = dv * lax.rsqrt(var + GN_EPS) * gn_w[:, sl] + gn_b[:, sl]
                bonus = half_sums(d["r"][:, sl] * d["k"][:, sl] * r_k[:, sl]) * d["v"][:, sl]
                o_ref[bi, c * cs:(c + 1) * cs, sl] = ((yn + bonus) * d["g"][:, sl]).astype(BF16)

    def on(c, fn, *args):
        if 0 <= c < n_chunks:
            fn(c, *args)

    for s in range(n_chunks + 3):
        on(s - 1, t_levels, 1)
        on(s - 2, chain_state_in)
        on(s, shift_and_maps)
        on(s - 1, t_levels, 2)
        on(s - 2, chain_solve)
        on(s, decay_operands)
        on(s - 1, t_levels, 2)
        on(s - 2, chain_out)
        on(s - 3, finish)
        on(s - 1, t_levels, n_levels - 5)
        on(s, score_matrices)
    for bi in range(n_batch):
        prev_ref[bi] = rw_ref[bi, rows - SUBLANES:rows, :]
    for i in range(n_heads):
        st_ref[i] = state[i]


def _rwkv_call(rw, mu, w_up, a_up, g_up, prm):
    b, lp, _ = rw.shape
    rows = RWKV_CHUNK * next(n for n in RWKV_CHUNKS_PER_STEP if lp % (RWKV_CHUNK * n) == 0)
    return pl.pallas_call(
        _rwkv_kernel,
        grid=(lp // rows,),
        in_specs=[
            pl.BlockSpec((b, rows, RW_COLS), lambda c: (0, c, 0)),
            _const_spec((1, RW_COLS)),
            _const_spec((LANES, W_MIX)),
            _const_spec((LANES, W_MIX)),
            _const_spec((LANES, W_MIX)),
            _const_spec((SUBLANES, W_MIX)),
        ],
        out_specs=pl.BlockSpec((b, rows, W_MIX), lambda c: (0, c, 0)),
        out_shape=jax.ShapeDtypeStruct((b, lp, W_MIX), BF16),
        scratch_shapes=[
            pltpu.VMEM((b * N_HEADS, 2 * RWKV_CHUNK, LANES), F32),
            pltpu.VMEM((b, SUBLANES, RW_COLS), F32),
        ],
        compiler_params=pltpu.CompilerParams(
            dimension_semantics=("arbitrary",), vmem_limit_bytes=VMEM_LIMIT),
        name="rwkv7",
    )(rw, mu, w_up, a_up, g_up, prm)


def _merge_kernel(h_ref, g_ref, of_ref, or_ref, wg_ref, gb_ref, wbf_ref, wbr_ref, wo_ref, out_ref):
    x = h_ref[...]
    hn = _rms_norm_rows(x, g_ref[...]).astype(BF16)
    gates = jax.nn.sigmoid(_dot(hn, wg_ref[...]) + gb_ref[...])
    merged = (gates[:, 0:D_MODEL] * _dot(of_ref[...], wbf_ref[...])
              + gates[:, D_MODEL:2 * D_MODEL] * _dot(or_ref[...], wbr_ref[...]))
    out_ref[...] = x + _dot(merged.astype(BF16), wo_ref[...])


def _merge_call(h, gain, o_fox, o_rwkv, w_g, g_bias, w_bf, w_br, w_o, *, tm):
    b, lp, _ = h.shape
    row = lambda n: pl.BlockSpec((None, tm, n), lambda bi, i: (bi, i, 0))
    return pl.pallas_call(
        _merge_kernel,
        grid=(b, lp // tm),
        in_specs=[
            row(D_MODEL), _const_spec((1, D_MODEL)), row(W_MIX), row(W_MIX),
            _const_spec((D_MODEL, 2 * D_MODEL)), _const_spec((1, 2 * D_MODEL)),
            _const_spec((W_MIX, D_MODEL)), _const_spec((W_MIX, D_MODEL)),
            _const_spec((D_MODEL, D_MODEL)),
        ],
        out_specs=row(D_MODEL),
        out_shape=jax.ShapeDtypeStruct(h.shape, F32),
        compiler_params=pltpu.CompilerParams(
            dimension_semantics=("parallel", "parallel"), vmem_limit_bytes=VMEM_LIMIT),
        name="merge",
    )(h, gain, o_fox, o_rwkv, w_g, g_bias, w_bf, w_br, w_o)


FF_CHUNK = D_FF // 2


def _ffn_kernel(h_ref, g_ref, wup_ref, wc_ref, wd_ref, out_ref, carry_ref, *, tm):
    @pl.when(pl.program_id(1) == 0)
    def _():
        carry_ref[...] = jnp.zeros_like(carry_ref)

    x = h_ref[...]
    hn = _rms_norm_rows(x, g_ref[...]).astype(BF16)
    wc = wc_ref[...]
    row = lax.broadcasted_iota(jnp.int32, (tm, FF_CHUNK), 0)

    def conv(u, col):
        prev = carry_ref[:, col:col + FF_CHUNK]
        p1 = prev[SUBLANES - 1:SUBLANES, :]
        p2 = prev[SUBLANES - 2:SUBLANES - 1, :]
        u1 = jnp.where(row == 0, p1, pltpu.roll(u, 1, 0))
        u2 = jnp.where(row == 0, p2, jnp.where(row == 1, p1, pltpu.roll(u, 2, 0)))
        carry_ref[:, col:col + FF_CHUNK] = u[tm - SUBLANES:tm, :]
        w = wc[:, col:col + FF_CHUNK]
        return w[0:1] * u2 + w[1:2] * u1 + w[2:3] * u

    acc = x
    for c in range(D_FF // FF_CHUNK):
        cg = c * FF_CHUNK
        cv = D_FF + c * FF_CHUNK
        gate = conv(_dot(hn, wup_ref[:, cg:cg + FF_CHUNK]), cg)
        val = conv(_dot(hn, wup_ref[:, cv:cv + FF_CHUNK]), cv)
        act = (gate * jax.nn.sigmoid(gate) * val).astype(BF16)
        acc = acc + _dot(act, wd_ref[cg:cg + FF_CHUNK, :])
    out_ref[...] = acc


def _ffn_call(h, gain, w_up, w_conv, w_down, *, tm):
    b, lp, _ = h.shape
    row = pl.BlockSpec((None, tm, D_MODEL), lambda bi, i: (bi, i, 0))
    return pl.pallas_call(
        functools.partial(_ffn_kernel, tm=tm),
        grid=(b, lp // tm),
        in_specs=[
            row, _const_spec((1, D_MODEL)),
            _const_spec((D_MODEL, 2 * D_FF)), _const_spec((CONV_W, 2 * D_FF)),
            _const_spec((D_FF, D_MODEL)),
        ],
        out_specs=row,
        out_shape=jax.ShapeDtypeStruct(h.shape, F32),
        scratch_shapes=[pltpu.VMEM((SUBLANES, 2 * D_FF), F32)],
        compiler_params=pltpu.CompilerParams(
            dimension_semantics=("parallel", "arbitrary"), vmem_limit_bytes=VMEM_LIMIT),
        name="conv_ffn",
    )(h, gain, w_up, w_conv, w_down)


def _row_tile(lp):
    for t in (640, 512, 384, 256, 128):
        if lp % t == 0:
            return t
    raise ValueError(f"padded length {lp} has no supported row tile")


def _pad_cols(w, n):
    return jnp.pad(w, [(0, 0)] * (w.ndim - 1) + [(0, n - w.shape[-1])])


def _prep_in_weights(w_in, f_bias, mu):
    o_f = 3 * W_MIX
    o_rw = o_f + N_HEADS
    n_rw = 3 * W_MIX + LORA_W + LORA_A + LORA_G
    o_g = o_rw + n_rw
    rw = w_in[..., o_rw:o_g]

    def rw_layout(t):
        return jnp.concatenate([
            t[..., 0:3 * W_MIX],
            _pad_cols(t[..., 3 * W_MIX:3 * W_MIX + LORA_W], LANES),
            _pad_cols(t[..., 3 * W_MIX + LORA_W:3 * W_MIX + LORA_W + LORA_A], LANES),
            t[..., 3 * W_MIX + LORA_W + LORA_A:],
        ], axis=-1)

    w_a = jnp.concatenate([
        w_in[..., 0:o_f],
        _pad_cols(w_in[..., o_f:o_rw], LANES),
        rw_layout(rw),
    ], axis=-1).astype(BF16)
    w_vt = jnp.swapaxes(w_in[..., 2 * W_MIX:3 * W_MIX], -1, -2).astype(BF16)
    w_g = w_in[..., o_g:].astype(BF16)
    fb = _pad_cols(f_bias, LANES)[:, None, :]
    return w_a, w_vt, w_g, fb, rw_layout(mu)[:, None, :]


def kernel(x, meta_tokens, norm_mix, norm_ffn, w_in, fox_q_norm, fox_k_norm, fox_f_bias,
           rwkv_shift_mu, rwkv_w0, rwkv_w_up, rwkv_a0, rwkv_a_up, rwkv_g_up, rwkv_k_k,
           rwkv_k_a, rwkv_r_k, rwkv_gn_w, rwkv_gn_b, w_branch_fox, w_branch_rwkv,
           gate_bias, w_out, ffn_up, ffn_conv, ffn_down):
    b, s, _ = x.shape
    depth = w_in.shape[0]
    length = N_META + s
    lp = -(-length // LANES) * LANES
    tm = _row_tile(lp)

    meta = jnp.broadcast_to(meta_tokens.astype(x.dtype)[None], (b, N_META, D_MODEL))
    h = jnp.concatenate([meta, x], axis=1)
    h = jnp.pad(h, ((0, 0), (0, lp - length), (0, 0)))

    w_a, w_vt, w_g, f_b, mu = _prep_in_weights(w_in, fox_f_bias, rwkv_shift_mu)
    qk_gain = jnp.stack([jnp.tile(fox_q_norm, (1, N_HEADS)), jnp.tile(fox_k_norm, (1, N_HEADS))], axis=1)
    bd = (jnp.kron(jnp.eye(N_HEADS, dtype=F32), jnp.ones((HEAD_DIM, HEAD_DIM), F32)) / HEAD_DIM).astype(BF16)
    pad_rows = lambda w: jnp.pad(w, ((0, 0), (0, LANES - w.shape[1]), (0, 0))).astype(BF16)
    w_up_l, a_up_l, g_up_l = pad_rows(rwkv_w_up), pad_rows(rwkv_a_up), rwkv_g_up.astype(BF16)
    prm = jnp.stack([rwkv_w0, rwkv_a0, rwkv_k_k, rwkv_k_a, rwkv_r_k.reshape(depth, W_MIX),
                     rwkv_gn_w, rwkv_gn_b, jnp.zeros_like(rwkv_w0)], axis=1)
    w_bf, w_br, w_o = w_branch_fox.astype(BF16), w_branch_rwkv.astype(BF16), w_out.astype(BF16)
    f_up, f_down = ffn_up.astype(BF16), ffn_down.astype(BF16)
    g_mix, g_ffn, g_bias = norm_mix[:, None, :], norm_ffn[:, None, :], gate_bias[:, None, :]

    for i in range(depth):
        q, kx, vt, rw = _in_call(h, g_mix[i], w_a[i], w_vt[i], qk_gain[i], bd, f_b[i], tm=tm)
        o_fox = _fox_call(q, kx, vt, tq=tm)
        o_rwkv = _rwkv_call(rw, mu[i], w_up_l[i], a_up_l[i], g_up_l[i], prm[i])
        h = _merge_call(h, g_mix[i], o_fox, o_rwkv, w_g[i], g_bias[i], w_bf[i], w_br[i], w_o[i], tm=tm)
        h = _ffn_call(h, g_ffn[i], f_up[i], ffn_conv[i], f_down[i], tm=tm)
    return h[:, N_META:N_META + s]
```

```python
import functools
import math

import jax
import jax.numpy as jnp
from jax import lax
from jax.experimental import pallas as pl
from jax.experimental.pallas import tpu as pltpu

D_MODEL = 1024
N_META = 16
HEAD_DIM = 64
N_HEADS = 8
W_MIX = N_HEADS * HEAD_DIM
LORA_W = 64
LORA_A = 64
LORA_G = 128
D_FF = 2816
CONV_W = 3
NORM_EPS = 1e-6
GN_EPS = HEAD_DIM * 1e-5
DEPTH = 4

LANES = 128
SUBLANES = 8
VMEM_LIMIT = 56 * 1024 * 1024

C_Q = 0
C_K = C_Q + W_MIX
C_F = C_K + W_MIX
C_RW = C_F + LANES
RW_COLS = 3 * W_MIX + 3 * LANES
N_A = C_RW + RW_COLS
Z_W = 3 * W_MIX
Z_A = Z_W + LANES
Z_G = Z_A + LANES

RWKV_CHUNK = 64
RWKV_CHUNKS_PER_STEP = (5, 4, 2, 1)
EXP_M05 = math.exp(-0.5)
LOG2E = math.log2(math.e)
N_PAIRS = W_MIX // LANES
FOX_TILE = LANES
GATE_TERMS = 3
FOX_L_ROWS = 16

BF16 = jnp.bfloat16
F32 = jnp.float32


def _dot(a, b):
    return jnp.dot(a, b, preferred_element_type=F32)


def _dot_nt(a, b):
    return lax.dot_general(a, b, (((1,), (1,)), ((), ())), preferred_element_type=F32)


def _rms_norm_rows(x, gain):
    ms = jnp.mean(x * x, axis=-1, keepdims=True)
    return x * lax.rsqrt(ms + NORM_EPS) * gain


def _split3_bf16(x):
    h1 = x.astype(BF16)
    r1 = x - h1.astype(F32)
    h2 = r1.astype(BF16)
    h3 = (r1 - h2.astype(F32)).astype(BF16)
    return h1, h2, h3


def _tril_ones(n, dtype):
    r = lax.broadcasted_iota(jnp.int32, (n, n), 0)
    c = lax.broadcasted_iota(jnp.int32, (n, n), 1)
    return (r >= c).astype(dtype)


def _const_spec(shape):
    return pl.BlockSpec(shape, lambda *_: (0,) * len(shape), pipeline_mode=pl.Buffered(1))


def _in_kernel(h_ref, g_ref, w_ref, wvt_ref, qkg_ref, bd_ref, fb_ref, sel_ref,
               q_ref, kx_ref, vt_ref, rw_ref, carry_ref, *, tm):
    @pl.when(pl.program_id(1) == 0)
    def _():
        carry_ref[...] = jnp.zeros_like(carry_ref)

    hn = _rms_norm_rows(h_ref[...], g_ref[...]).astype(BF16)
    bd = bd_ref[...]

    def head_norm(t, gain):
        ms = _dot((t * t).astype(BF16), bd)
        return t * lax.rsqrt(ms + NORM_EPS) * gain

    q = _dot(hn, w_ref[:, C_Q:C_Q + W_MIX])
    q_ref[...] = (head_norm(q, qkg_ref[0:1, :]) * (HEAD_DIM ** -0.5 * LOG2E)).astype(BF16)
    k = head_norm(_dot(hn, w_ref[:, C_K:C_K + W_MIX]), qkg_ref[1:2, :]).astype(BF16)
    for p in range(N_PAIRS):
        kx_ref[:, 2 * p * LANES:(2 * p + 1) * LANES] = k[:, p * LANES:(p + 1) * LANES]
    vt_ref[...] = _dot_nt(wvt_ref[...], hn).astype(BF16)
    rw_ref[...] = _dot(hn, w_ref[:, C_RW:C_RW + RW_COLS])

    f = _dot(hn, w_ref[:, C_F:C_F + LANES]) + fb_ref[...]
    log_f = jnp.minimum(f, 0.0) - jnp.log1p(jnp.exp(-jnp.abs(f)))
    tri = _tril_ones(LANES, BF16)
    carry = carry_ref[0:1, :]
    for blk in range(tm // LANES):
        rows = slice(blk * LANES, (blk + 1) * LANES)
        h1, h2, h3 = _split3_bf16(log_f[rows])
        c = _dot(tri, h1) + _dot(tri, h2) + _dot(tri, h3) + carry
        b1, b2, b3 = _split3_bf16(c * (-LOG2E))
        bias = (_dot(b1, sel_ref[0]) + _dot(b2, sel_ref[1]) + _dot(b3, sel_ref[2])).astype(BF16)
        for p in range(N_PAIRS):
            kx_ref[rows, (2 * p + 1) * LANES:(2 * p + 2) * LANES] = bias
        carry = c[LANES - 1:LANES, :]
    carry_ref[...] = jnp.broadcast_to(carry, carry_ref.shape)


def _gate_selectors():
    sel = [[[0.0] * LANES for _ in range(LANES)] for _ in range(GATE_TERMS)]
    for h in range(N_HEADS):
        for term in range(GATE_TERMS):
            sel[term][h][GATE_TERMS * h + term] = 1.0
    return jnp.asarray(sel, dtype=BF16)


def _in_call(h, gain, w_a, w_vt, qk_gain, bd, f_bias, *, tm):
    b, lp, _ = h.shape
    grid = (b, lp // tm)
    row = lambda n: pl.BlockSpec((None, tm, n), lambda bi, i: (bi, i, 0))
    return pl.pallas_call(
        functools.partial(_in_kernel, tm=tm),
        grid=grid,
        in_specs=[
            row(D_MODEL),
            _const_spec((1, D_MODEL)),
            _const_spec((D_MODEL, N_A)),
            _const_spec((W_MIX, D_MODEL)),
            _const_spec((2, W_MIX)),
            _const_spec((W_MIX, W_MIX)),
            _const_spec((1, LANES)),
            _const_spec((GATE_TERMS, LANES, LANES)),
        ],
        out_specs=[
            row(W_MIX), row(2 * W_MIX),
            pl.BlockSpec((None, W_MIX, tm), lambda bi, i: (bi, 0, i)),
            row(RW_COLS),
        ],
        out_shape=[
            jax.ShapeDtypeStruct((b, lp, W_MIX), BF16),
            jax.ShapeDtypeStruct((b, lp, 2 * W_MIX), BF16),
            jax.ShapeDtypeStruct((b, W_MIX, lp), BF16),
            jax.ShapeDtypeStruct((b, lp, RW_COLS), F32),
        ],
        scratch_shapes=[pltpu.VMEM((SUBLANES, LANES), F32)],
        compiler_params=pltpu.CompilerParams(
            dimension_semantics=("parallel", "arbitrary"), vmem_limit_bytes=VMEM_LIMIT),
        name="in_proj",
    )(h, gain, w_a, w_vt, qk_gain, bd, f_bias, _gate_selectors())


def _fox_kernel(q_ref, kx_ref, vt_ref, o_ref, qx_ref, m_ref, acc_ref, *, tq):
    i = pl.program_id(2)
    nt = tq // FOX_TILE
    q_t = q_ref[...].astype(F32).T
    row = lax.broadcasted_iota(jnp.int32, (LANES, FOX_TILE), 0)
    first = row < HEAD_DIM
    g0 = 2 * GATE_TERMS * pl.program_id(1)
    gate_a = jnp.where(jnp.logical_and(row >= g0, row < g0 + GATE_TERMS), 1.0, 0.0).astype(BF16)
    gate_b = jnp.where(jnp.logical_and(row >= g0 + GATE_TERMS, row < g0 + 2 * GATE_TERMS),
                       1.0, 0.0).astype(BF16)
    for t in range(nt):
        qt = q_t[:, t * FOX_TILE:(t + 1) * FOX_TILE]
        c0 = 2 * t * FOX_TILE
        qx_ref[0:LANES, c0:c0 + FOX_TILE] = jnp.where(first, qt, 0.0).astype(BF16)
        qx_ref[LANES:2 * LANES, c0:c0 + FOX_TILE] = gate_a
        qx_ref[0:LANES, c0 + FOX_TILE:c0 + 2 * FOX_TILE] = jnp.where(first, 0.0, qt).astype(BF16)
        qx_ref[LANES:2 * LANES, c0 + FOX_TILE:c0 + 2 * FOX_TILE] = gate_b
    m_ref[...] = jnp.full_like(m_ref, -jnp.inf)
    acc_ref[...] = jnp.zeros_like(acc_ref)

    def scores(t, kx):
        return _dot(kx, qx_ref[:, 2 * t * FOX_TILE:2 * (t + 1) * FOX_TILE])

    def update(tiles, ss, vt):
        m_prev = [m_ref[t] for t in tiles]
        m_new = [jnp.maximum(mp, jnp.max(s, axis=0, keepdims=True)) for mp, s in zip(m_prev, ss)]
        alpha = [jnp.exp2(mp - mn) for mp, mn in zip(m_prev, m_new)]
        ps = [jnp.exp2((s - mn).astype(BF16)) for s, mn in zip(ss, m_new)]
        vt_ones = jnp.concatenate([vt, jnp.ones((FOX_L_ROWS, vt.shape[1]), BF16)], axis=0)
        pv = [_dot(vt_ones, p) for p in ps]
        for n, t in enumerate(tiles):
            m_ref[t] = m_new[n]
            acc_ref[t] = alpha[n] * acc_ref[t] + pv[n]

    def causal_mask(s, rel, t):
        if rel is None or rel + s.shape[0] - 1 <= t * FOX_TILE:
            return s
        key_i = lax.broadcasted_iota(jnp.int32, s.shape, 0) + rel
        qry_i = (lax.broadcasted_iota(jnp.int32, s.shape, 1) & (FOX_TILE - 1)) + t * FOX_TILE
        return jnp.where(key_i <= qry_i, s, -jnp.inf)

    def run(items):
        def issue(n):
            start, size, t0, _ = items[n]
            kx = kx_ref[pl.ds(start, size), :]
            return [scores(t, kx) for t in range(t0, nt)]

        pending = issue(0)
        for n, (start, size, t0, rel) in enumerate(items):
            upcoming = issue(n + 1) if n + 1 < len(items) else None
            vt = vt_ref[:, pl.ds(start, size)]
            tiles = list(range(t0, nt))
            update(tiles, [causal_mask(s, rel, t) for t, s in zip(tiles, pending)], vt)
            pending = upcoming

    def span_items(j, n_blocks):
        base = pl.multiple_of(j * tq, LANES)
        return [(base + off, size, 0, None) for off, size in _key_chunks(n_blocks * tq)]

    def pair_body(jj, carry):
        run(span_items(2 * jj, 2))
        return carry

    lax.fori_loop(0, lax.shift_right_logical(i, 1), pair_body, 0)

    @pl.when((i & 1) == 1)
    def _():
        run(span_items(i - 1, 1))

    base = pl.multiple_of(i * tq, LANES)
    run([(base + off, size, off // FOX_TILE, off) for off, size in _key_chunks(tq)])

    dim_i = lax.broadcasted_iota(jnp.int32, (LANES, FOX_TILE), 0)
    for t in range(nt):
        o = acc_ref[t, 0:LANES, :] / acc_ref[t, LANES:LANES + 1, :]
        own = jnp.where(dim_i < HEAD_DIM, o[:, 0:FOX_TILE], o[:, FOX_TILE:2 * FOX_TILE])
        o_ref[t * FOX_TILE:(t + 1) * FOX_TILE, :] = own.T.astype(BF16)


def _key_chunks(tq):
    chunks, off = [], 0
    while off < tq:
        size = min(2 * LANES, tq - off)
        chunks.append((off, size))
        off += size
    return chunks


def _fox_call(q, kx, vt, *, tq):
    b, lp, _ = q.shape
    nt = tq // FOX_TILE
    return pl.pallas_call(
        functools.partial(_fox_kernel, tq=tq),
        grid=(b, N_PAIRS, lp // tq),
        in_specs=[
            pl.BlockSpec((None, tq, LANES), lambda bi, hp, i: (bi, i, hp)),
            pl.BlockSpec((None, lp, 2 * LANES), lambda bi, hp, i: (bi, 0, hp)),
            pl.BlockSpec((None, LANES, lp), lambda bi, hp, i: (bi, hp, 0)),
        ],
        out_specs=pl.BlockSpec((None, tq, LANES), lambda bi, hp, i: (bi, i, hp)),
        out_shape=jax.ShapeDtypeStruct((b, lp, W_MIX), BF16),
        scratch_shapes=[
            pltpu.VMEM((2 * LANES, 2 * tq), BF16),
            pltpu.VMEM((nt, 1, 2 * FOX_TILE), F32),
            pltpu.VMEM((nt, LANES + FOX_L_ROWS, 2 * FOX_TILE), F32),
        ],
        compiler_params=pltpu.CompilerParams(
            dimension_semantics=("parallel", "parallel", "arbitrary"),
            vmem_limit_bytes=VMEM_LIMIT),
        name="fox_attention",
    )(q, kx, vt)


def _rwkv_kernel(rw_ref, mu_ref, wup_ref, aup_ref, gup_ref, p_ref, o_ref, st_ref, prev_ref):
    cs = RWKV_CHUNK
    n_batch, rows, _ = rw_ref.shape
    n_chunks = rows // cs
    n_levels = int(math.log2(cs))

    @pl.when(pl.program_id(0) == 0)
    def _():
        st_ref[...] = jnp.zeros_like(st_ref)
        prev_ref[...] = jnp.zeros_like(prev_ref)

    prm = p_ref[...]
    w0, a0, k_k, k_a = prm[0:1], prm[1:2], prm[2:3], prm[3:4]
    r_k, gn_w, gn_b = prm[4:5], prm[5:6], prm[6:7]
    tri = _tril_ones(cs, BF16)

    lane = lax.broadcasted_iota(jnp.int32, (cs, LANES), 1)
    rowi = lax.broadcasted_iota(jnp.int32, (cs, LANES), 0)
    lo = lane < HEAD_DIM
    hi = jnp.logical_not(lo)
    col = lane & (HEAD_DIM - 1)
    strict = rowi > col
    incl = rowi >= col
    eye_hi = jnp.where(jnp.logical_and(hi, rowi == col), 1.0, 0.0)
    zeros_blk = jnp.zeros((cs, LANES), BF16)
    row_full = lax.broadcasted_iota(jnp.int32, (cs, RW_COLS), 0)

    def half_sums(t):
        s_lo = jnp.sum(jnp.where(lo, t, 0.0), axis=-1, keepdims=True)
        s_hi = jnp.sum(jnp.where(hi, t, 0.0), axis=-1, keepdims=True)
        return jnp.where(lo, s_lo, s_hi)

    heads = [(bi, h) for bi in range(n_batch) for h in range(N_HEADS)]
    n_heads = len(heads)
    pre = {}

    def shift_and_maps(c):
        for bi in range(n_batch):
            rw = rw_ref[bi, c * cs:(c + 1) * cs, :]
            before = (prev_ref[bi, SUBLANES - 1:SUBLANES, :] if c == 0
                      else rw_ref[bi, c * cs - 1:c * cs, :])
            rw_prev = jnp.where(row_full == 0, before, pltpu.roll(rw, 1, 0))
            z = rw + mu_ref[...] * (rw_prev - rw)
            r_all = z[:, 0:W_MIX]
            kr = z[:, W_MIX:2 * W_MIX]
            v_all = z[:, 2 * W_MIX:3 * W_MIX]
            zw = z[:, Z_W:Z_W + LANES]
            za = z[:, Z_A:Z_A + LANES]
            zg = z[:, Z_G:Z_G + LANES]
            lw = -EXP_M05 * jax.nn.sigmoid(w0 + _dot(jnp.tanh(zw).astype(BF16), wup_ref[...]))
            a_rate = jax.nn.sigmoid(a0 + _dot(za.astype(BF16), aup_ref[...]))
            g_all = _dot(jax.nn.sigmoid(zg).astype(BF16), gup_ref[...])
            k_all = kr * (1.0 + (a_rate - 1.0) * k_a)
            kk = kr * k_k
            kk = jnp.concatenate(
                [kk[:, p * LANES:(p + 1) * LANES]
                 / jnp.maximum(jnp.sqrt(half_sums(jnp.square(kk[:, p * LANES:(p + 1) * LANES]))), 1e-12)
                 for p in range(N_PAIRS)], axis=1)
            pre[c, bi] = dict(r=r_all, k=k_all, v=v_all, g=g_all, lw=lw, kk=kk, b=kk * a_rate)

    def decay_operands(c):
        for bi in range(n_batch):
            d = pre[c, bi]
            lw, kk, b_all, r_all, k_all, v_all = d["lw"], d["kk"], d["b"], d["r"], d["k"], d["v"]
            l1, l2, l3 = _split3_bf16(lw)
            cw = _dot(tri, l1) + _dot(tri, l2) + _dot(tri, l3)
            cw_last = cw[cs - 1:cs, :]
            c_ref = 0.5 * cw_last
            e_cw = jnp.exp(cw)
            e_cwm = jnp.exp(cw - lw)
            e_nref = jnp.exp(-c_ref)
            e_neg = jnp.exp(c_ref - cw)
            e_last = jnp.exp(cw_last - cw)
            d.update(
                v_roll=pltpu.roll(v_all, HEAD_DIM, 1),
                at=-kk * (e_cwm * e_nref), rt=r_all * (e_cw * e_nref),
                bt=b_all * e_neg, kt=k_all * e_neg,
                a0=-kk * e_cwm, r0=r_all * e_cw,
                bh=b_all * e_last, kh=k_all * e_last,
                e_total=jnp.exp(cw_last))

    def grp(c, bi, h, name):
        p = h // 2
        return pre[c, bi][name][:, p * LANES:(p + 1) * LANES]

    def mine(h):
        return hi if h % 2 else lo

    def stack_masked(c, bi, h, top, bot):
        t = jnp.concatenate([grp(c, bi, h, top), grp(c, bi, h, bot)], axis=0)
        m = jnp.concatenate([mine(h), mine(h)], axis=0)
        return jnp.where(m, t, 0.0).astype(BF16)

    m_bot, v_hi, x_v, w = {}, {}, {}, {}

    def score_matrices(c):
        m_all = [_dot_nt(stack_masked(c, bi, h, "at", "rt"),
                         jnp.concatenate([grp(c, bi, h, "bt"), grp(c, bi, h, "kt")], axis=0).astype(BF16))
                 for bi, h in heads]
        m_top = [jnp.where(strict, m[0:cs], 0.0) for m in m_all]
        m_bot[c] = [jnp.where(incl, m[cs:2 * cs], 0.0).astype(BF16) for m in m_all]
        v_hi[c] = [jnp.where(hi, grp(c, bi, h, "v" if h % 2 else "v_roll"), 0.0) for bi, h in heads]
        x_v[c] = [_dot(m_top[i].astype(BF16), jnp.concatenate([zeros_blk, v_hi[c][i].astype(BF16)], axis=0))
                  for i in range(n_heads)]
        w[c] = [jnp.where(lo, m_top[i], eye_hi) for i in range(n_heads)]

    def t_levels(c, count):
        for _ in range(count):
            pw = [_dot(w[c][i][:, 0:HEAD_DIM].astype(BF16), w[c][i].astype(BF16)) for i in range(n_heads)]
            w[c] = [pw[i] + jnp.where(lo, 0.0, w[c][i]) for i in range(n_heads)]

    state = [st_ref[i] for i in range(n_heads)]
    s_in, u_hi, y = {}, {}, {}

    def chain_state_in(c):
        s_in[c] = [_dot_nt(stack_masked(c, bi, h, "a0", "r0"), state[i].astype(BF16))
                   for i, (bi, h) in enumerate(heads)]

    def chain_solve(c):
        t_hi = [jnp.where(lo, 0.0, w[c][i]).astype(BF16) for i in range(n_heads)]
        u_hi[c] = [_dot(t_hi[i], jnp.concatenate([zeros_blk, (s_in[c][i][0:cs] + x_v[c][i]).astype(BF16)], axis=0))
                   for i in range(n_heads)]

    def chain_out(c):
        z32 = [jnp.concatenate([u_hi[c][i], v_hi[c][i]], axis=0) for i in range(n_heads)]
        for i, (bi, h) in enumerate(heads):
            state[i] = (state[i] * grp(c, bi, h, "e_total")
                        + _dot(z32[i].T.astype(BF16), stack_masked(c, bi, h, "bh", "kh")))
        y[c] = [s_in[c][i][cs:2 * cs] + _dot(m_bot[c][i], z32[i].astype(BF16)) for i in range(n_heads)]

    def finish(c):
        for bi in range(n_batch):
            d = pre[c, bi]
            for p in range(N_PAIRS):
                i_even = bi * N_HEADS + 2 * p
                sl = slice(p * LANES, (p + 1) * LANES)
                yp = pltpu.roll(y[c][i_even], HEAD_DIM, 1) + y[c][i_even + 1]
                mean = half_sums(yp) * (1.0 / HEAD_DIM)
                dv = yp - mean
                var = half_sums(dv * dv) * (1.0 / HEAD_DIM)
                yn = dv * lax.rsqrt(var + GN_EPS) * gn_w[:, sl] + gn_b[:, sl]
                bonus = half_sums(d["r"][:, sl] * d["k"][:, sl] * r_k[:, sl]) * d["v"][:, sl]
                o_ref[bi, c * cs:(c + 1) * cs, sl] = ((yn + bonus) * d["g"][:, sl]).astype(BF16)

    def on(c, fn, *args):
        if 0 <= c < n_chunks:
            fn(c, *args)

    for s in range(n_chunks + 3):
        on(s - 1, t_levels, 1)
        on(s - 2, chain_state_in)
        on(s, shift_and_maps)
        on(s - 1, t_levels, 2)
        on(s - 2, chain_solve)
        on(s, decay_operands)
        on(s - 1, t_levels, 2)
        on(s - 2, chain_out)
        on(s - 3, finish)
        on(s - 1, t_levels, n_levels - 5)
        on(s, score_matrices)
    for bi in range(n_batch):
        prev_ref[bi] = rw_ref[bi, rows - SUBLANES:rows, :]
    for i in range(n_heads):
        st_ref[i] = state[i]


def _rwkv_call(rw, mu, w_up, a_up, g_up, prm):
    b, lp, _ = rw.shape
    rows = RWKV_CHUNK * next(n for n in RWKV_CHUNKS_PER_STEP if lp % (RWKV_CHUNK * n) == 0)
    return pl.pallas_call(
        _rwkv_kernel,
        grid=(lp // rows,),
        in_specs=[
            pl.BlockSpec((b, rows, RW_COLS), lambda c: (0, c, 0)),
            _const_spec((1, RW_COLS)),
            _const_spec((LANES, W_MIX)),
            _const_spec((LANES, W_MIX)),
            _const_spec((LANES, W_MIX)),
            _const_spec((SUBLANES, W_MIX)),
        ],
        out_specs=pl.BlockSpec((b, rows, W_MIX), lambda c: (0, c, 0)),
        out_shape=jax.ShapeDtypeStruct((b, lp, W_MIX), BF16),
        scratch_shapes=[
            pltpu.VMEM((b * N_HEADS, 2 * RWKV_CHUNK, LANES), F32),
            pltpu.VMEM((b, SUBLANES, RW_COLS), F32),
        ],
        compiler_params=pltpu.CompilerParams(
            dimension_semantics=("arbitrary",), vmem_limit_bytes=VMEM_LIMIT),
        name="rwkv7",
    )(rw, mu, w_up, a_up, g_up, prm)


def _merge_kernel(h_ref, g_ref, of_ref, or_ref, wg_ref, gb_ref, wbf_ref, wbr_ref, wo_ref, out_ref):
    x = h_ref[...]
    hn = _rms_norm_rows(x, g_ref[...]).astype(BF16)
    gates = jax.nn.sigmoid(_dot(hn, wg_ref[...]) + gb_ref[...])
    merged = (gates[:, 0:D_MODEL] * _dot(of_ref[...], wbf_ref[...])
              + gates[:, D_MODEL:2 * D_MODEL] * _dot(or_ref[...], wbr_ref[...]))
    out_ref[...] = x + _dot(merged.astype(BF16), wo_ref[...])


def _merge_call(h, gain, o_fox, o_rwkv, w_g, g_bias, w_bf, w_br, w_o, *, tm):
    b, lp, _ = h.shape
    row = lambda n: pl.BlockSpec((None, tm, n), lambda bi, i: (bi, i, 0))
    return pl.pallas_call(
        _merge_kernel,
        grid=(b, lp // tm),
        in_specs=[
            row(D_MODEL), _const_spec((1, D_MODEL)), row(W_MIX), row(W_MIX),
            _const_spec((D_MODEL, 2 * D_MODEL)), _const_spec((1, 2 * D_MODEL)),
            _const_spec((W_MIX, D_MODEL)), _const_spec((W_MIX, D_MODEL)),
            _const_spec((D_MODEL, D_MODEL)),
        ],
        out_specs=row(D_MODEL),
        out_shape=jax.ShapeDtypeStruct(h.shape, F32),
        compiler_params=pltpu.CompilerParams(
            dimension_semantics=("parallel", "parallel"), vmem_limit_bytes=VMEM_LIMIT),
        name="merge",
    )(h, gain, o_fox, o_rwkv, w_g, g_bias, w_bf, w_br, w_o)


FF_CHUNK = D_FF // 2


def _ffn_kernel(h_ref, g_ref, wup_ref, wc_ref, wd_ref, out_ref, carry_ref, *, tm):
    @pl.when(pl.program_id(1) == 0)
    def _():
        carry_ref[...] = jnp.zeros_like(carry_ref)

    x = h_ref[...]
    hn = _rms_norm_rows(x, g_ref[...]).astype(BF16)
    wc = wc_ref[...]
    row = lax.broadcasted_iota(jnp.int32, (tm, FF_CHUNK), 0)

    def conv(u, col):
        prev = carry_ref[:, col:col + FF_CHUNK]
        p1 = prev[SUBLANES - 1:SUBLANES, :]
        p2 = prev[SUBLANES - 2:SUBLANES - 1, :]
        u1 = jnp.where(row == 0, p1, pltpu.roll(u, 1, 0))
        u2 = jnp.where(row == 0, p2, jnp.where(row == 1, p1, pltpu.roll(u, 2, 0)))
        carry_ref[:, col:col + FF_CHUNK] = u[tm - SUBLANES:tm, :]
        w = wc[:, col:col + FF_CHUNK]
        return w[0:1] * u2 + w[1:2] * u1 + w[2:3] * u

    acc = x
    for c in range(D_FF // FF_CHUNK):
        cg = c * FF_CHUNK
        cv = D_FF + c * FF_CHUNK
        gate = conv(_dot(hn, wup_ref[:, cg:cg + FF_CHUNK]), cg)
        val = conv(_dot(hn, wup_ref[:, cv:cv + FF_CHUNK]), cv)
        act = (gate * jax.nn.sigmoid(gate) * val).astype(BF16)
        acc = acc + _dot(act, wd_ref[cg:cg + FF_CHUNK, :])
    out_ref[...] = acc


def _ffn_call(h, gain, w_up, w_conv, w_down, *, tm):
    b, lp, _ = h.shape
    row = pl.BlockSpec((None, tm, D_MODEL), lambda bi, i: (bi, i, 0))
    return pl.pallas_call(
        functools.partial(_ffn_kernel, tm=tm),
        grid=(b, lp // tm),
        in_specs=[
            row, _const_spec((1, D_MODEL)),
            _const_spec((D_MODEL, 2 * D_FF)), _const_spec((CONV_W, 2 * D_FF)),
            _const_spec((D_FF, D_MODEL)),
        ],
        out_specs=row,
        out_shape=jax.ShapeDtypeStruct(h.shape, F32),
        scratch_shapes=[pltpu.VMEM((SUBLANES, 2 * D_FF), F32)],
        compiler_params=pltpu.CompilerParams(
            dimension_semantics=("parallel", "arbitrary"), vmem_limit_bytes=VMEM_LIMIT),
        name="conv_ffn",
    )(h, gain, w_up, w_conv, w_down)


def _row_tile(lp):
    for t in (640, 512, 384, 256, 128):
        if lp % t == 0:
            return t
    raise ValueError(f"padded length {lp} has no supported row tile")


def _pad_cols(w, n):
    return jnp.pad(w, [(0, 0)] * (w.ndim - 1) + [(0, n - w.shape[-1])])


def _prep_in_weights(w_in, f_bias, mu):
    o_f = 3 * W_MIX
    o_rw = o_f + N_HEADS
    n_rw = 3 * W_MIX + LORA_W + LORA_A + LORA_G
    o_g = o_rw + n_rw
    w_bf = w_in.astype(BF16)

    def rw_layout(t):
        return jnp.concatenate([
            t[..., 0:3 * W_MIX],
            _pad_cols(t[..., 3 * W_MIX:3 * W_MIX + LORA_W], LANES),
            _pad_cols(t[..., 3 * W_MIX + LORA_W:3 * W_MIX + LORA_W + LORA_A], LANES),
            t[..., 3 * W_MIX + LORA_W + LORA_A:],
        ], axis=-1)

    w_a = jnp.concatenate([
        w_bf[..., 0:2 * W_MIX],
        _pad_cols(w_bf[..., o_f:o_rw], LANES),
        rw_layout(w_bf[..., o_rw:o_g]),
    ], axis=-1)
    w_vt = jnp.swapaxes(w_bf[..., 2 * W_MIX:3 * W_MIX], -1, -2)
    w_g = w_bf[..., o_g:]
    fb = _pad_cols(f_bias, LANES)[:, None, :]
    return w_a, w_vt, w_g, fb, rw_layout(mu)[:, None, :]


def kernel(x, meta_tokens, norm_mix, norm_ffn, w_in, fox_q_norm, fox_k_norm, fox_f_bias,
           rwkv_shift_mu, rwkv_w0, rwkv_w_up, rwkv_a0, rwkv_a_up, rwkv_g_up, rwkv_k_k,
           rwkv_k_a, rwkv_r_k, rwkv_gn_w, rwkv_gn_b, w_branch_fox, w_branch_rwkv,
           gate_bias, w_out, ffn_up, ffn_conv, ffn_down):
    b, s, _ = x.shape
    depth = w_in.shape[0]
    length = N_META + s
    lp = -(-length // LANES) * LANES
    tm = _row_tile(lp)

    meta = jnp.broadcast_to(meta_tokens.astype(x.dtype)[None], (b, N_META, D_MODEL))
    h = jnp.concatenate([meta, x], axis=1)
    h = jnp.pad(h, ((0, 0), (0, lp - length), (0, 0)))

    w_a, w_vt, w_g, f_b, mu = _prep_in_weights(w_in, fox_f_bias, rwkv_shift_mu)
    qk_gain = jnp.stack([jnp.tile(fox_q_norm, (1, N_HEADS)), jnp.tile(fox_k_norm, (1, N_HEADS))], axis=1)
    bd = (jnp.kron(jnp.eye(N_HEADS, dtype=F32), jnp.ones((HEAD_DIM, HEAD_DIM), F32)) / HEAD_DIM).astype(BF16)
    pad_rows = lambda w: jnp.pad(w, ((0, 0), (0, LANES - w.shape[1]), (0, 0))).astype(BF16)
    w_up_l, a_up_l, g_up_l = pad_rows(rwkv_w_up), pad_rows(rwkv_a_up), rwkv_g_up.astype(BF16)
    prm = jnp.stack([rwkv_w0, rwkv_a0, rwkv_k_k, rwkv_k_a, rwkv_r_k.reshape(depth, W_MIX),
                     rwkv_gn_w, rwkv_gn_b, jnp.zeros_like(rwkv_w0)], axis=1)
    w_bf, w_br, w_o = w_branch_fox.astype(BF16), w_branch_rwkv.astype(BF16), w_out.astype(BF16)
    f_up, f_down = ffn_up.astype(BF16), ffn_down.astype(BF16)
    g_mix, g_ffn, g_bias = norm_mix[:, None, :], norm_ffn[:, None, :], gate_bias[:, None, :]

    for i in range(depth):
        q, kx, vt, rw = _in_call(h, g_mix[i], w_a[i], w_vt[i], qk_gain[i], bd, f_b[i], tm=tm)
        o_fox = _fox_call(q, kx, vt, tq=tm)
        o_rwkv = _rwkv_call(rw, mu[i], w_up_l[i], a_up_l[i], g_up_l[i], prm[i])
        h = _merge_call(h, g_mix[i], o_fox, o_rwkv, w_g[i], g_bias[i], w_bf[i], w_br[i], w_o[i], tm=tm)
        h = _ffn_call(h, g_ffn[i], f_up[i], ffn_conv[i], f_down[i], tm=tm)
    return h[:, N_META:N_META + s]
```

```python
import functools
import math

import jax
import jax.numpy as jnp
from jax import lax
from jax.experimental import pallas as pl
from jax.experimental.pallas import tpu as pltpu

D_MODEL = 1024
N_META = 16
HEAD_DIM = 64
N_HEADS = 8
W_MIX = N_HEADS * HEAD_DIM
LORA_W = 64
LORA_A = 64
LORA_G = 128
D_FF = 2816
CONV_W = 3
NORM_EPS = 1e-6
GN_EPS = HEAD_DIM * 1e-5
DEPTH = 4

LANES = 128
SUBLANES = 8
VMEM_LIMIT = 56 * 1024 * 1024

C_Q = 0
C_K = C_Q + W_MIX
C_F = C_K + W_MIX
C_RW = C_F + LANES
RW_COLS = 3 * W_MIX + 3 * LANES
N_A = C_RW + RW_COLS
Z_W = 3 * W_MIX
Z_A = Z_W + LANES
Z_G = Z_A + LANES

RWKV_CHUNK = 64
RWKV_CHUNKS_PER_STEP = (5, 4, 2, 1)
EXP_M05 = math.exp(-0.5)
LOG2E = math.log2(math.e)
N_PAIRS = W_MIX // LANES
FOX_TILE = LANES
GATE_TERMS = 3
FOX_L_ROWS = 16

BF16 = jnp.bfloat16
F32 = jnp.float32


def _dot(a, b):
    return jnp.dot(a, b, preferred_element_type=F32)


def _dot_nt(a, b):
    return lax.dot_general(a, b, (((1,), (1,)), ((), ())), preferred_element_type=F32)


def _rms_norm_rows(x, gain):
    ms = jnp.mean(x * x, axis=-1, keepdims=True)
    return x * lax.rsqrt(ms + NORM_EPS) * gain


def _split3_bf16(x):
    h1 = x.astype(BF16)
    r1 = x - h1.astype(F32)
    h2 = r1.astype(BF16)
    h3 = (r1 - h2.astype(F32)).astype(BF16)
    return h1, h2, h3


def _tril_ones(n, dtype):
    r = lax.broadcasted_iota(jnp.int32, (n, n), 0)
    c = lax.broadcasted_iota(jnp.int32, (n, n), 1)
    return (r >= c).astype(dtype)


def _const_spec(shape, layer=None):
    if layer is None:
        return pl.BlockSpec(shape, lambda *_: (0,) * len(shape), pipeline_mode=pl.Buffered(1))
    return pl.BlockSpec((None,) + tuple(shape), lambda *_: (layer,) + (0,) * len(shape),
                        pipeline_mode=pl.Buffered(1))


def _in_kernel(h_ref, g_ref, w_ref, wvt_ref, qkg_ref, bd_ref, fb_ref, sel_ref,
               q_ref, kx_ref, vt_ref, rw_ref, carry_ref, *, tm):
    @pl.when(pl.program_id(1) == 0)
    def _():
        carry_ref[...] = jnp.zeros_like(carry_ref)

    hn = _rms_norm_rows(h_ref[...], g_ref[...]).astype(BF16)
    bd = bd_ref[...]

    def head_norm(t, gain):
        ms = _dot((t * t).astype(BF16), bd)
        return t * lax.rsqrt(ms + NORM_EPS) * gain

    q = _dot(hn, w_ref[:, C_Q:C_Q + W_MIX])
    q_ref[...] = (head_norm(q, qkg_ref[0:1, :]) * (HEAD_DIM ** -0.5 * LOG2E)).astype(BF16)
    k = head_norm(_dot(hn, w_ref[:, C_K:C_K + W_MIX]), qkg_ref[1:2, :]).astype(BF16)
    for p in range(N_PAIRS):
        kx_ref[:, 2 * p * LANES:(2 * p + 1) * LANES] = k[:, p * LANES:(p + 1) * LANES]
    vt_ref[...] = _dot_nt(wvt_ref[...], hn).astype(BF16)
    rw_ref[...] = _dot(hn, w_ref[:, C_RW:C_RW + RW_COLS])

    f = _dot(hn, w_ref[:, C_F:C_F + LANES]) + fb_ref[...]
    log_f = jnp.minimum(f, 0.0) - jnp.log1p(jnp.exp(-jnp.abs(f)))
    tri = _tril_ones(LANES, BF16)
    carry = carry_ref[0:1, :]
    for blk in range(tm // LANES):
        rows = slice(blk * LANES, (blk + 1) * LANES)
        h1, h2, h3 = _split3_bf16(log_f[rows])
        c = _dot(tri, h1) + _dot(tri, h2) + _dot(tri, h3) + carry
        b1, b2, b3 = _split3_bf16(c * (-LOG2E))
        bias = (_dot(b1, sel_ref[0]) + _dot(b2, sel_ref[1]) + _dot(b3, sel_ref[2])).astype(BF16)
        for p in range(N_PAIRS):
            kx_ref[rows, (2 * p + 1) * LANES:(2 * p + 2) * LANES] = bias
        carry = c[LANES - 1:LANES, :]
    carry_ref[...] = jnp.broadcast_to(carry, carry_ref.shape)


def _gate_selectors():
    sel = [[[0.0] * LANES for _ in range(LANES)] for _ in range(GATE_TERMS)]
    for h in range(N_HEADS):
        for term in range(GATE_TERMS):
            sel[term][h][GATE_TERMS * h + term] = 1.0
    return jnp.asarray(sel, dtype=BF16)


def _in_call(h, gain, w_a, w_vt, qk_gain, bd, f_bias, *, layer, tm):
    b, lp, _ = h.shape
    grid = (b, lp // tm)
    row = lambda n: pl.BlockSpec((None, tm, n), lambda bi, i: (bi, i, 0))
    return pl.pallas_call(
        functools.partial(_in_kernel, tm=tm),
        grid=grid,
        in_specs=[
            row(D_MODEL),
            _const_spec((1, D_MODEL), layer),
            _const_spec((D_MODEL, N_A), layer),
            _const_spec((W_MIX, D_MODEL), layer),
            _const_spec((2, W_MIX), layer),
            _const_spec((W_MIX, W_MIX)),
            _const_spec((1, LANES), layer),
            _const_spec((GATE_TERMS, LANES, LANES)),
        ],
        out_specs=[
            row(W_MIX), row(2 * W_MIX),
            pl.BlockSpec((None, W_MIX, tm), lambda bi, i: (bi, 0, i)),
            row(RW_COLS),
        ],
        out_shape=[
            jax.ShapeDtypeStruct((b, lp, W_MIX), BF16),
            jax.ShapeDtypeStruct((b, lp, 2 * W_MIX), BF16),
            jax.ShapeDtypeStruct((b, W_MIX, lp), BF16),
            jax.ShapeDtypeStruct((b, lp, RW_COLS), F32),
        ],
        scratch_shapes=[pltpu.VMEM((SUBLANES, LANES), F32)],
        compiler_params=pltpu.CompilerParams(
            dimension_semantics=("parallel", "arbitrary"), vmem_limit_bytes=VMEM_LIMIT),
        name="in_proj",
    )(h, gain, w_a, w_vt, qk_gain, bd, f_bias, _gate_selectors())


def _fox_kernel(q_ref, kx_ref, vt_ref, o_ref, qx_ref, m_ref, acc_ref, *, tq):
    i = pl.program_id(2)
    nt = tq // FOX_TILE
    q_t = q_ref[...].astype(F32).T
    row = lax.broadcasted_iota(jnp.int32, (LANES, FOX_TILE), 0)
    first = row < HEAD_DIM
    g0 = 2 * GATE_TERMS * pl.program_id(1)
    gate_a = jnp.where(jnp.logical_and(row >= g0, row < g0 + GATE_TERMS), 1.0, 0.0).astype(BF16)
    gate_b = jnp.where(jnp.logical_and(row >= g0 + GATE_TERMS, row < g0 + 2 * GATE_TERMS),
                       1.0, 0.0).astype(BF16)
    for t in range(nt):
        qt = q_t[:, t * FOX_TILE:(t + 1) * FOX_TILE]
        c0 = 2 * t * FOX_TILE
        qx_ref[0:LANES, c0:c0 + FOX_TILE] = jnp.where(first, qt, 0.0).astype(BF16)
        qx_ref[LANES:2 * LANES, c0:c0 + FOX_TILE] = gate_a
        qx_ref[0:LANES, c0 + FOX_TILE:c0 + 2 * FOX_TILE] = jnp.where(first, 0.0, qt).astype(BF16)
        qx_ref[LANES:2 * LANES, c0 + FOX_TILE:c0 + 2 * FOX_TILE] = gate_b
    m_ref[...] = jnp.full_like(m_ref, -jnp.inf)
    acc_ref[...] = jnp.zeros_like(acc_ref)

    def scores(t, kx):
        return _dot(kx, qx_ref[:, 2 * t * FOX_TILE:2 * (t + 1) * FOX_TILE])

    def update(tiles, ss, vt):
        m_prev = [m_ref[t] for t in tiles]
        m_new = [jnp.maximum(mp, jnp.max(s, axis=0, keepdims=True)) for mp, s in zip(m_prev, ss)]
        alpha = [jnp.exp2(mp - mn) for mp, mn in zip(m_prev, m_new)]
        ps = [jnp.exp2((s - mn).astype(BF16)) for s, mn in zip(ss, m_new)]
        vt_ones = jnp.concatenate([vt, jnp.ones((FOX_L_ROWS, vt.shape[1]), BF16)], axis=0)
        pv = [_dot(vt_ones, p) for p in ps]
        for n, t in enumerate(tiles):
            m_ref[t] = m_new[n]
            acc_ref[t] = alpha[n] * acc_ref[t] + pv[n]

    def causal_mask(s, rel, t):
        if rel is None or rel + s.shape[0] - 1 <= t * FOX_TILE:
            return s
        key_i = lax.broadcasted_iota(jnp.int32, s.shape, 0) + rel
        qry_i = (lax.broadcasted_iota(jnp.int32, s.shape, 1) & (FOX_TILE - 1)) + t * FOX_TILE
        return jnp.where(key_i <= qry_i, s, -jnp.inf)

    def run(items):
        def issue(n):
            start, size, t0, _ = items[n]
            kx = kx_ref[pl.ds(start, size), :]
            return [scores(t, kx) for t in range(t0, nt)]

        pending = issue(0)
        for n, (start, size, t0, rel) in enumerate(items):
            upcoming = issue(n + 1) if n + 1 < len(items) else None
            vt = vt_ref[:, pl.ds(start, size)]
            tiles = list(range(t0, nt))
            update(tiles, [causal_mask(s, rel, t) for t, s in zip(tiles, pending)], vt)
            pending = upcoming

    def span_items(j, n_blocks):
        base = pl.multiple_of(j * tq, LANES)
        return [(base + off, size, 0, None) for off, size in _key_chunks(n_blocks * tq)]

    def pair_body(jj, carry):
        run(span_items(2 * jj, 2))
        return carry

    lax.fori_loop(0, lax.shift_right_logical(i, 1), pair_body, 0)

    @pl.when((i & 1) == 1)
    def _():
        run(span_items(i - 1, 1))

    base = pl.multiple_of(i * tq, LANES)
    run([(base + off, size, off // FOX_TILE, off) for off, size in _key_chunks(tq)])

    dim_i = lax.broadcasted_iota(jnp.int32, (LANES, FOX_TILE), 0)
    for t in range(nt):
        o = acc_ref[t, 0:LANES, :] / acc_ref[t, LANES:LANES + 1, :]
        own = jnp.where(dim_i < HEAD_DIM, o[:, 0:FOX_TILE], o[:, FOX_TILE:2 * FOX_TILE])
        o_ref[t * FOX_TILE:(t + 1) * FOX_TILE, :] = own.T.astype(BF16)


def _key_chunks(tq):
    chunks, off = [], 0
    while off < tq:
        size = min(2 * LANES, tq - off)
        chunks.append((off, size))
        off += size
    return chunks


def _fox_call(q, kx, vt, *, tq):
    b, lp, _ = q.shape
    nt = tq // FOX_TILE
    return pl.pallas_call(
        functools.partial(_fox_kernel, tq=tq),
        grid=(b, N_PAIRS, lp // tq),
        in_specs=[
            pl.BlockSpec((None, tq, LANES), lambda bi, hp, i: (bi, i, hp)),
            pl.BlockSpec((None, lp, 2 * LANES), lambda bi, hp, i: (bi, 0, hp)),
            pl.BlockSpec((None, LANES, lp), lambda bi, hp, i: (bi, hp, 0)),
        ],
        out_specs=pl.BlockSpec((None, tq, LANES), lambda bi, hp, i: (bi, i, hp)),
        out_shape=jax.ShapeDtypeStruct((b, lp, W_MIX), BF16),
        scratch_shapes=[
            pltpu.VMEM((2 * LANES, 2 * tq), BF16),
            pltpu.VMEM((nt, 1, 2 * FOX_TILE), F32),
            pltpu.VMEM((nt, LANES + FOX_L_ROWS, 2 * FOX_TILE), F32),
        ],
        compiler_params=pltpu.CompilerParams(
            dimension_semantics=("parallel", "parallel", "arbitrary"),
            vmem_limit_bytes=VMEM_LIMIT),
        name="fox_attention",
    )(q, kx, vt)


def _rwkv_kernel(rw_ref, mu_ref, wup_ref, aup_ref, gup_ref, p_ref, o_ref, st_ref, prev_ref):
    cs = RWKV_CHUNK
    n_batch, rows, _ = rw_ref.shape
    n_chunks = rows // cs
    n_levels = int(math.log2(cs))

    @pl.when(pl.program_id(0) == 0)
    def _():
        st_ref[...] = jnp.zeros_like(st_ref)
        prev_ref[...] = jnp.zeros_like(prev_ref)

    prm = p_ref[...]
    w0, a0, k_k, k_a = prm[0:1], prm[1:2], prm[2:3], prm[3:4]
    r_k, gn_w, gn_b = prm[4:5], prm[5:6], prm[6:7]
    tri = _tril_ones(cs, BF16)

    lane = lax.broadcasted_iota(jnp.int32, (cs, LANES), 1)
    rowi = lax.broadcasted_iota(jnp.int32, (cs, LANES), 0)
    lo = lane < HEAD_DIM
    hi = jnp.logical_not(lo)
    col = lane & (HEAD_DIM - 1)
    strict = rowi > col
    incl = rowi >= col
    eye_hi = jnp.where(jnp.logical_and(hi, rowi == col), 1.0, 0.0)
    zeros_blk = jnp.zeros((cs, LANES), BF16)
    row_full = lax.broadcasted_iota(jnp.int32, (cs, RW_COLS), 0)

    def half_sums(t):
        s_lo = jnp.sum(jnp.where(lo, t, 0.0), axis=-1, keepdims=True)
        s_hi = jnp.sum(jnp.where(hi, t, 0.0), axis=-1, keepdims=True)
        return jnp.where(lo, s_lo, s_hi)

    heads = [(bi, h) for bi in range(n_batch) for h in range(N_HEADS)]
    n_heads = len(heads)
    pre = {}

    def shift_and_maps(c):
        for bi in range(n_batch):
            rw = rw_ref[bi, c * cs:(c + 1) * cs, :]
            before = (prev_ref[bi, SUBLANES - 1:SUBLANES, :] if c == 0
                      else rw_ref[bi, c * cs - 1:c * cs, :])
            rw_prev = jnp.where(row_full == 0, before, pltpu.roll(rw, 1, 0))
            z = rw + mu_ref[...] * (rw_prev - rw)
            r_all = z[:, 0:W_MIX]
            kr = z[:, W_MIX:2 * W_MIX]
            v_all = z[:, 2 * W_MIX:3 * W_MIX]
            zw = z[:, Z_W:Z_W + LANES]
            za = z[:, Z_A:Z_A + LANES]
            zg = z[:, Z_G:Z_G + LANES]
            lw = -EXP_M05 * jax.nn.sigmoid(w0 + _dot(jnp.tanh(zw).astype(BF16), wup_ref[...]))
            a_rate = jax.nn.sigmoid(a0 + _dot(za.astype(BF16), aup_ref[...]))
            g_all = _dot(jax.nn.sigmoid(zg).astype(BF16), gup_ref[...])
            k_all = kr * (1.0 + (a_rate - 1.0) * k_a)
            kk = kr * k_k
            kk = jnp.concatenate(
                [kk[:, p * LANES:(p + 1) * LANES]
                 / jnp.maximum(jnp.sqrt(half_sums(jnp.square(kk[:, p * LANES:(p + 1) * LANES]))), 1e-12)
                 for p in range(N_PAIRS)], axis=1)
            pre[c, bi] = dict(r=r_all, k=k_all, v=v_all, g=g_all, lw=lw, kk=kk, b=kk * a_rate)

    def decay_operands(c):
        for bi in range(n_batch):
            d = pre[c, bi]
            lw, kk, b_all, r_all, k_all, v_all = d["lw"], d["kk"], d["b"], d["r"], d["k"], d["v"]
            l1, l2, l3 = _split3_bf16(lw)
            cw = _dot(tri, l1) + _dot(tri, l2) + _dot(tri, l3)
            cw_last = cw[cs - 1:cs, :]
            c_ref = 0.5 * cw_last
            e_cw = jnp.exp(cw)
            e_cwm = jnp.exp(cw - lw)
            e_nref = jnp.exp(-c_ref)
            e_neg = jnp.exp(c_ref - cw)
            e_last = jnp.exp(cw_last - cw)
            d.update(
                v_roll=pltpu.roll(v_all, HEAD_DIM, 1),
                at=-kk * (e_cwm * e_nref), rt=r_all * (e_cw * e_nref),
                bt=b_all * e_neg, kt=k_all * e_neg,
                a0=-kk * e_cwm, r0=r_all * e_cw,
                bh=b_all * e_last, kh=k_all * e_last,
                e_total=jnp.exp(cw_last))

    def grp(c, bi, h, name):
        p = h // 2
        return pre[c, bi][name][:, p * LANES:(p + 1) * LANES]

    def mine(h):
        return hi if h % 2 else lo

    def stack_masked(c, bi, h, top, bot):
        t = jnp.concatenate([grp(c, bi, h, top), grp(c, bi, h, bot)], axis=0)
        m = jnp.concatenate([mine(h), mine(h)], axis=0)
        return jnp.where(m, t, 0.0).astype(BF16)

    m_bot, v_hi, x_v, w = {}, {}, {}, {}

    def score_matrices(c):
        m_all = [_dot_nt(stack_masked(c, bi, h, "at", "rt"),
                         jnp.concatenate([grp(c, bi, h, "bt"), grp(c, bi, h, "kt")], axis=0).astype(BF16))
                 for bi, h in heads]
        m_top = [jnp.where(strict, m[0:cs], 0.0) for m in m_all]
        m_bot[c] = [jnp.where(incl, m[cs:2 * cs], 0.0).astype(BF16) for m in m_all]
        v_hi[c] = [jnp.where(hi, grp(c, bi, h, "v" if h % 2 else "v_roll"), 0.0) for bi, h in heads]
        x_v[c] = [_dot(m_top[i].astype(BF16), jnp.concatenate([zeros_blk, v_hi[c][i].astype(BF16)], axis=0))
                  for i in range(n_heads)]
        w[c] = [jnp.where(lo, m_top[i], eye_hi) for i in range(n_heads)]

    def t_levels(c, count):
        for _ in range(count):
            pw = [_dot(w[c][i][:, 0:HEAD_DIM].astype(BF16), w[c][i].astype(BF16)) for i in range(n_heads)]
            w[c] = [pw[i] + jnp.where(lo, 0.0, w[c][i]) for i in range(n_heads)]

    state = [st_ref[i] for i in range(n_heads)]
    s_in, u_hi, y = {}, {}, {}

    def chain_state_in(c):
        s_in[c] = [_dot_nt(stack_masked(c, bi, h, "a0", "r0"), state[i].astype(BF16))
                   for i, (bi, h) in enumerate(heads)]

    def chain_solve(c):
        t_hi = [jnp.where(lo, 0.0, w[c][i]).astype(BF16) for i in range(n_heads)]
        u_hi[c] = [_dot(t_hi[i], jnp.concatenate([zeros_blk, (s_in[c][i][0:cs] + x_v[c][i]).astype(BF16)], axis=0))
                   for i in range(n_heads)]

    def chain_out(c):
        z32 = [jnp.concatenate([u_hi[c][i], v_hi[c][i]], axis=0) for i in range(n_heads)]
        for i, (bi, h) in enumerate(heads):
            state[i] = (state[i] * grp(c, bi, h, "e_total")
                        + _dot(z32[i].T.astype(BF16), stack_masked(c, bi, h, "bh", "kh")))
        y[c] = [s_in[c][i][cs:2 * cs] + _dot(m_bot[c][i], z32[i].astype(BF16)) for i in range(n_heads)]

    def finish(c):
        for bi in range(n_batch):
            d = pre[c, bi]
            for p in range(N_PAIRS):
                i_even = bi * N_HEADS + 2 * p
                sl = slice(p * LANES, (p + 1) * LANES)
                yp = pltpu.roll(y[c][i_even], HEAD_DIM, 1) + y[c][i_even + 1]
                mean = half_sums(yp) * (1.0 / HEAD_DIM)
                dv = yp - mean
                var = half_sums(dv * dv) * (1.0 / HEAD_DIM)
                yn = dv * lax.rsqrt(var + GN_EPS) * gn_w[:, sl] + gn_b[:, sl]
                bonus = half_sums(d["r"][:, sl] * d["k"][:, sl] * r_k[:, sl]) * d["v"][:, sl]
                o_ref[bi, c * cs:(c + 1) * cs, sl] = ((yn + bonus) * d["g"][:, sl]).astype(BF16)

    def on(c, fn, *args):
        if 0 <= c < n_chunks:
            fn(c, *args)

    for s in range(n_chunks + 3):
        on(s - 1, t_levels, 1)
        on(s - 2, chain_state_in)
        on(s, shift_and_maps)
        on(s - 1, t_levels, 2)
        on(s - 2, chain_solve)
        on(s, decay_operands)
        on(s - 1, t_levels, 2)
        on(s - 2, chain_out)
        on(s - 3, finish)
        on(s - 1, t_levels, n_levels - 5)
        on(s, score_matrices)
    for bi in range(n_batch):
        prev_ref[bi] = rw_ref[bi, rows - SUBLANES:rows, :]
    for i in range(n_heads):
        st_ref[i] = state[i]


def _rwkv_call(rw, mu, w_up, a_up, g_up, prm, *, layer):
    b, lp, _ = rw.shape
    rows = RWKV_CHUNK * next(n for n in RWKV_CHUNKS_PER_STEP if lp % (RWKV_CHUNK * n) == 0)
    return pl.pallas_call(
        _rwkv_kernel,
        grid=(lp // rows,),
        in_specs=[
            pl.BlockSpec((b, rows, RW_COLS), lambda c: (0, c, 0)),
            _const_spec((1, RW_COLS), layer),
            _const_spec((LANES, W_MIX), layer),
            _const_spec((LANES, W_MIX), layer),
            _const_spec((LANES, W_MIX), layer),
            _const_spec((SUBLANES, W_MIX), layer),
        ],
        out_specs=pl.BlockSpec((b, rows, W_MIX), lambda c: (0, c, 0)),
        out_shape=jax.ShapeDtypeStruct((b, lp, W_MIX), BF16),
        scratch_shapes=[
            pltpu.VMEM((b * N_HEADS, 2 * RWKV_CHUNK, LANES), F32),
            pltpu.VMEM((b, SUBLANES, RW_COLS), F32),
        ],
        compiler_params=pltpu.CompilerParams(
            dimension_semantics=("arbitrary",), vmem_limit_bytes=VMEM_LIMIT),
        name="rwkv7",
    )(rw, mu, w_up, a_up, g_up, prm)


def _merge_kernel(h_ref, g_ref, of_ref, or_ref, wg_ref, gb_ref, wbf_ref, wbr_ref, wo_ref, out_ref):
    x = h_ref[...]
    hn = _rms_norm_rows(x, g_ref[...]).astype(BF16)
    gates = jax.nn.sigmoid(_dot(hn, wg_ref[...]) + gb_ref[...])
    merged = (gates[:, 0:D_MODEL] * _dot(of_ref[...], wbf_ref[...])
              + gates[:, D_MODEL:2 * D_MODEL] * _dot(or_ref[...], wbr_ref[...]))
    out_ref[...] = x + _dot(merged.astype(BF16), wo_ref[...])


def _merge_call(h, gain, o_fox, o_rwkv, w_g, g_bias, w_bf, w_br, w_o, *, layer, tm):
    b, lp, _ = h.shape
    row = lambda n: pl.BlockSpec((None, tm, n), lambda bi, i: (bi, i, 0))
    return pl.pallas_call(
        _merge_kernel,
        grid=(b, lp // tm),
        in_specs=[
            row(D_MODEL), _const_spec((1, D_MODEL), layer), row(W_MIX), row(W_MIX),
            _const_spec((D_MODEL, 2 * D_MODEL), layer), _const_spec((1, 2 * D_MODEL), layer),
            _const_spec((W_MIX, D_MODEL), layer), _const_spec((W_MIX, D_MODEL), layer),
            _const_spec((D_MODEL, D_MODEL), layer),
        ],
        out_specs=row(D_MODEL),
        out_shape=jax.ShapeDtypeStruct(h.shape, F32),
        compiler_params=pltpu.CompilerParams(
            dimension_semantics=("parallel", "parallel"), vmem_limit_bytes=VMEM_LIMIT),
        name="merge",
    )(h, gain, o_fox, o_rwkv, w_g, g_bias, w_bf, w_br, w_o)


FF_CHUNK = D_FF // 2


def _ffn_kernel(h_ref, g_ref, wup_ref, wc_ref, wd_ref, out_ref, carry_ref, *, tm):
    @pl.when(pl.program_id(1) == 0)
    def _():
        carry_ref[...] = jnp.zeros_like(carry_ref)

    x = h_ref[...]
    hn = _rms_norm_rows(x, g_ref[...]).astype(BF16)
    wc = wc_ref[...]
    row = lax.broadcasted_iota(jnp.int32, (tm, FF_CHUNK), 0)

    def conv(u, col):
        prev = carry_ref[:, col:col + FF_CHUNK]
        p1 = prev[SUBLANES - 1:SUBLANES, :]
        p2 = prev[SUBLANES - 2:SUBLANES - 1, :]
        u1 = jnp.where(row == 0, p1, pltpu.roll(u, 1, 0))
        u2 = jnp.where(row == 0, p2, jnp.where(row == 1, p1, pltpu.roll(u, 2, 0)))
        carry_ref[:, col:col + FF_CHUNK] = u[tm - SUBLANES:tm, :]
        w = wc[:, col:col + FF_CHUNK]
        return w[0:1] * u2 + w[1:2] * u1 + w[2:3] * u

    acc = x
    for c in range(D_FF // FF_CHUNK):
        cg = c * FF_CHUNK
        cv = D_FF + c * FF_CHUNK
        gate = conv(_dot(hn, wup_ref[:, cg:cg + FF_CHUNK]), cg)
        val = conv(_dot(hn, wup_ref[:, cv:cv + FF_CHUNK]), cv)
        act = (gate * jax.nn.sigmoid(gate) * val).astype(BF16)
        acc = acc + _dot(act, wd_ref[cg:cg + FF_CHUNK, :])
    out_ref[...] = acc


def _ffn_call(h, gain, w_up, w_conv, w_down, *, layer, tm):
    b, lp, _ = h.shape
    row = pl.BlockSpec((None, tm, D_MODEL), lambda bi, i: (bi, i, 0))
    return pl.pallas_call(
        functools.partial(_ffn_kernel, tm=tm),
        grid=(b, lp // tm),
        in_specs=[
            row, _const_spec((1, D_MODEL), layer),
            _const_spec((D_MODEL, 2 * D_FF), layer), _const_spec((CONV_W, 2 * D_FF), layer),
            _const_spec((D_FF, D_MODEL), layer),
        ],
        out_specs=row,
        out_shape=jax.ShapeDtypeStruct(h.shape, F32),
        scratch_shapes=[pltpu.VMEM((SUBLANES, 2 * D_FF), F32)],
        compiler_params=pltpu.CompilerParams(
            dimension_semantics=("parallel", "arbitrary"), vmem_limit_bytes=VMEM_LIMIT),
        name="conv_ffn",
    )(h, gain, w_up, w_conv, w_down)


def _row_tile(lp):
    for t in (640, 512, 384, 256, 128):
        if lp % t == 0:
            return t
    raise ValueError(f"padded length {lp} has no supported row tile")


def _pad_cols(w, n):
    return jnp.pad(w, [(0, 0)] * (w.ndim - 1) + [(0, n - w.shape[-1])])


def _prep_in_weights(w_in, f_bias, mu):
    o_f = 3 * W_MIX
    o_rw = o_f + N_HEADS
    n_rw = 3 * W_MIX + LORA_W + LORA_A + LORA_G
    o_g = o_rw + n_rw
    w_bf = w_in.astype(BF16)

    def rw_layout(t):
        return jnp.concatenate([
            t[..., 0:3 * W_MIX],
            _pad_cols(t[..., 3 * W_MIX:3 * W_MIX + LORA_W], LANES),
            _pad_cols(t[..., 3 * W_MIX + LORA_W:3 * W_MIX + LORA_W + LORA_A], LANES),
            t[..., 3 * W_MIX + LORA_W + LORA_A:],
        ], axis=-1)

    w_a = jnp.concatenate([
        w_bf[..., 0:2 * W_MIX],
        _pad_cols(w_bf[..., o_f:o_rw], LANES),
        rw_layout(w_bf[..., o_rw:o_g]),
    ], axis=-1)
    w_vt = jnp.swapaxes(w_bf[..., 2 * W_MIX:3 * W_MIX], -1, -2)
    w_g = w_bf[..., o_g:]
    fb = _pad_cols(f_bias, LANES)[:, None, :]
    return w_a, w_vt, w_g, fb, rw_layout(mu)[:, None, :]


def kernel(x, meta_tokens, norm_mix, norm_ffn, w_in, fox_q_norm, fox_k_norm, fox_f_bias,
           rwkv_shift_mu, rwkv_w0, rwkv_w_up, rwkv_a0, rwkv_a_up, rwkv_g_up, rwkv_k_k,
           rwkv_k_a, rwkv_r_k, rwkv_gn_w, rwkv_gn_b, w_branch_fox, w_branch_rwkv,
           gate_bias, w_out, ffn_up, ffn_conv, ffn_down):
    b, s, _ = x.shape
    depth = w_in.shape[0]
    length = N_META + s
    lp = -(-length // LANES) * LANES
    tm = _row_tile(lp)

    meta = jnp.broadcast_to(meta_tokens.astype(x.dtype)[None], (b, N_META, D_MODEL))
    h = jnp.concatenate([meta, x], axis=1)
    h = jnp.pad(h, ((0, 0), (0, lp - length), (0, 0)))

    w_a, w_vt, w_g, f_b, mu = _prep_in_weights(w_in, fox_f_bias, rwkv_shift_mu)
    qk_gain = jnp.stack([jnp.tile(fox_q_norm, (1, N_HEADS)), jnp.tile(fox_k_norm, (1, N_HEADS))], axis=1)
    bd = (jnp.kron(jnp.eye(N_HEADS, dtype=F32), jnp.ones((HEAD_DIM, HEAD_DIM), F32)) / HEAD_DIM).astype(BF16)
    pad_rows = lambda w: jnp.pad(w, ((0, 0), (0, LANES - w.shape[1]), (0, 0))).astype(BF16)
    w_up_l, a_up_l, g_up_l = pad_rows(rwkv_w_up), pad_rows(rwkv_a_up), rwkv_g_up.astype(BF16)
    prm = jnp.stack([rwkv_w0, rwkv_a0, rwkv_k_k, rwkv_k_a, rwkv_r_k.reshape(depth, W_MIX),
                     rwkv_gn_w, rwkv_gn_b, jnp.zeros_like(rwkv_w0)], axis=1)
    w_bf, w_br, w_o = w_branch_fox.astype(BF16), w_branch_rwkv.astype(BF16), w_out.astype(BF16)
    f_up, f_down = ffn_up.astype(BF16), ffn_down.astype(BF16)
    g_mix, g_ffn, g_bias = norm_mix[:, None, :], norm_ffn[:, None, :], gate_bias[:, None, :]

    for i in range(depth):
        q, kx, vt, rw = _in_call(h, g_mix, w_a, w_vt, qk_gain, bd, f_b, layer=i, tm=tm)
        o_fox = _fox_call(q, kx, vt, tq=tm)
        o_rwkv = _rwkv_call(rw, mu, w_up_l, a_up_l, g_up_l, prm, layer=i)
        h = _merge_call(h, g_mix, o_fox, o_rwkv, w_g, g_bias, w_bf, w_br, w_o, layer=i, tm=tm)
        h = _ffn_call(h, g_ffn, f_up, ffn_conv, f_down, layer=i, tm=tm)
    return h[:, N_META:N_META + s]
```

```python
import functools
import math

import jax
import jax.numpy as jnp
from jax import lax
from jax.experimental import pallas as pl
from jax.experimental.pallas import tpu as pltpu

D_MODEL = 1024
N_META = 16
HEAD_DIM = 64
N_HEADS = 8
W_MIX = N_HEADS * HEAD_DIM
LORA_W = 64
LORA_A = 64
LORA_G = 128
D_FF = 2816
CONV_W = 3
NORM_EPS = 1e-6
GN_EPS = HEAD_DIM * 1e-5
DEPTH = 4

LANES = 128
SUBLANES = 8
VMEM_LIMIT = 56 * 1024 * 1024

C_Q = 0
C_K = C_Q + W_MIX
C_F = C_K + W_MIX
C_RW = C_F + LANES
RW_COLS = 3 * W_MIX + 3 * LANES
N_A = C_RW + RW_COLS
Z_W = 3 * W_MIX
Z_A = Z_W + LANES
Z_G = Z_A + LANES

RWKV_CHUNK = 64
RWKV_CHUNKS_PER_STEP = (5, 4, 2, 1)
EXP_M05 = math.exp(-0.5)
LOG2E = math.log2(math.e)
N_PAIRS = W_MIX // LANES
FOX_TILE = LANES
GATE_TERMS = 3
FOX_L_ROWS = 16

BF16 = jnp.bfloat16
F32 = jnp.float32


def _dot(a, b):
    return jnp.dot(a, b, preferred_element_type=F32)


def _dot_nt(a, b):
    return lax.dot_general(a, b, (((1,), (1,)), ((), ())), preferred_element_type=F32)


def _rms_norm_rows(x, gain):
    ms = jnp.mean(x * x, axis=-1, keepdims=True)
    return x * lax.rsqrt(ms + NORM_EPS) * gain


def _split3_bf16(x):
    h1 = x.astype(BF16)
    r1 = x - h1.astype(F32)
    h2 = r1.astype(BF16)
    h3 = (r1 - h2.astype(F32)).astype(BF16)
    return h1, h2, h3


def _tril_ones(n, dtype):
    r = lax.broadcasted_iota(jnp.int32, (n, n), 0)
    c = lax.broadcasted_iota(jnp.int32, (n, n), 1)
    return (r >= c).astype(dtype)


def _const_spec(shape, layer=None):
    if layer is None:
        return pl.BlockSpec(shape, lambda *_: (0,) * len(shape), pipeline_mode=pl.Buffered(1))
    return pl.BlockSpec((None,) + tuple(shape), lambda *_: (layer,) + (0,) * len(shape),
                        pipeline_mode=pl.Buffered(1))


def _in_kernel(h_ref, g_ref, w_ref, wvt_ref, qkg_ref, bd_ref, fb_ref, sel_ref,
               q_ref, kx_ref, vt_ref, rw_ref, carry_ref, *, tm):
    @pl.when(pl.program_id(1) == 0)
    def _():
        carry_ref[...] = jnp.zeros_like(carry_ref)

    hn = _rms_norm_rows(h_ref[...], g_ref[...]).astype(BF16)
    bd = bd_ref[...]

    def head_norm(t, gain):
        ms = _dot((t * t).astype(BF16), bd)
        return t * lax.rsqrt(ms + NORM_EPS) * gain

    q = _dot(hn, w_ref[:, C_Q:C_Q + W_MIX])
    q_ref[...] = (head_norm(q, qkg_ref[0:1, :]) * (HEAD_DIM ** -0.5 * LOG2E)).astype(BF16)
    k = head_norm(_dot(hn, w_ref[:, C_K:C_K + W_MIX]), qkg_ref[1:2, :]).astype(BF16)
    for p in range(N_PAIRS):
        kx_ref[:, 2 * p * LANES:(2 * p + 1) * LANES] = k[:, p * LANES:(p + 1) * LANES]
    vt_ref[...] = _dot_nt(wvt_ref[...], hn).astype(BF16)
    rw_ref[...] = _dot(hn, w_ref[:, C_RW:C_RW + RW_COLS])

    f = _dot(hn, w_ref[:, C_F:C_F + LANES]) + fb_ref[...]
    log_f = jnp.minimum(f, 0.0) - jnp.log1p(jnp.exp(-jnp.abs(f)))
    tri = _tril_ones(LANES, BF16)
    carry = carry_ref[0:1, :]
    for blk in range(tm // LANES):
        rows = slice(blk * LANES, (blk + 1) * LANES)
        h1, h2, h3 = _split3_bf16(log_f[rows])
        c = _dot(tri, h1) + _dot(tri, h2) + _dot(tri, h3) + carry
        b1, b2, b3 = _split3_bf16(c * (-LOG2E))
        bias = (_dot(b1, sel_ref[0]) + _dot(b2, sel_ref[1]) + _dot(b3, sel_ref[2])).astype(BF16)
        for p in range(N_PAIRS):
            kx_ref[rows, (2 * p + 1) * LANES:(2 * p + 2) * LANES] = bias
        carry = c[LANES - 1:LANES, :]
    carry_ref[...] = jnp.broadcast_to(carry, carry_ref.shape)


def _gate_selectors():
    sel = [[[0.0] * LANES for _ in range(LANES)] for _ in range(GATE_TERMS)]
    for h in range(N_HEADS):
        for term in range(GATE_TERMS):
            sel[term][h][GATE_TERMS * h + term] = 1.0
    return jnp.asarray(sel, dtype=BF16)


def _in_call(h, gain, w_a, w_vt, qk_gain, bd, f_bias, *, layer, tm):
    b, lp, _ = h.shape
    grid = (b, lp // tm)
    row = lambda n: pl.BlockSpec((None, tm, n), lambda bi, i: (bi, i, 0))
    return pl.pallas_call(
        functools.partial(_in_kernel, tm=tm),
        grid=grid,
        in_specs=[
            row(D_MODEL),
            _const_spec((1, D_MODEL), layer),
            _const_spec((D_MODEL, N_A), layer),
            _const_spec((W_MIX, D_MODEL), layer),
            _const_spec((2, W_MIX), layer),
            _const_spec((W_MIX, W_MIX)),
            _const_spec((1, LANES), layer),
            _const_spec((GATE_TERMS, LANES, LANES)),
        ],
        out_specs=[
            row(W_MIX), row(2 * W_MIX),
            pl.BlockSpec((None, W_MIX, tm), lambda bi, i: (bi, 0, i)),
            row(RW_COLS),
        ],
        out_shape=[
            jax.ShapeDtypeStruct((b, lp, W_MIX), BF16),
            jax.ShapeDtypeStruct((b, lp, 2 * W_MIX), BF16),
            jax.ShapeDtypeStruct((b, W_MIX, lp), BF16),
            jax.ShapeDtypeStruct((b, lp, RW_COLS), F32),
        ],
        scratch_shapes=[pltpu.VMEM((SUBLANES, LANES), F32)],
        compiler_params=pltpu.CompilerParams(
            dimension_semantics=("parallel", "arbitrary"), vmem_limit_bytes=VMEM_LIMIT),
        name="in_proj",
    )(h, gain, w_a, w_vt, qk_gain, bd, f_bias, _gate_selectors())


def _fox_kernel(q_ref, kx_ref, vt_ref, o_ref, qx_ref, m_ref, acc_ref, *, tq):
    i = pl.program_id(2)
    nt = tq // FOX_TILE
    q_t = q_ref[...].astype(F32).T
    row = lax.broadcasted_iota(jnp.int32, (LANES, FOX_TILE), 0)
    first = row < HEAD_DIM
    g0 = 2 * GATE_TERMS * pl.program_id(1)
    gate_a = jnp.where(jnp.logical_and(row >= g0, row < g0 + GATE_TERMS), 1.0, 0.0).astype(BF16)
    gate_b = jnp.where(jnp.logical_and(row >= g0 + GATE_TERMS, row < g0 + 2 * GATE_TERMS),
                       1.0, 0.0).astype(BF16)
    for t in range(nt):
        qt = q_t[:, t * FOX_TILE:(t + 1) * FOX_TILE]
        c0 = 2 * t * FOX_TILE
        qx_ref[0:LANES, c0:c0 + FOX_TILE] = jnp.where(first, qt, 0.0).astype(BF16)
        qx_ref[LANES:2 * LANES, c0:c0 + FOX_TILE] = gate_a
        qx_ref[0:LANES, c0 + FOX_TILE:c0 + 2 * FOX_TILE] = jnp.where(first, 0.0, qt).astype(BF16)
        qx_ref[LANES:2 * LANES, c0 + FOX_TILE:c0 + 2 * FOX_TILE] = gate_b
    m_ref[...] = jnp.full_like(m_ref, -jnp.inf)
    acc_ref[...] = jnp.zeros_like(acc_ref)

    def scores(t, kx):
        return _dot(kx, qx_ref[:, 2 * t * FOX_TILE:2 * (t + 1) * FOX_TILE])

    def update(tiles, ss, vt):
        m_prev = [m_ref[t] for t in tiles]
        m_new = [jnp.maximum(mp, jnp.max(s, axis=0, keepdims=True)) for mp, s in zip(m_prev, ss)]
        alpha = [jnp.exp2(mp - mn) for mp, mn in zip(m_prev, m_new)]
        ps = [jnp.exp2((s - mn).astype(BF16)) for s, mn in zip(ss, m_new)]
        vt_ones = jnp.concatenate([vt, jnp.ones((FOX_L_ROWS, vt.shape[1]), BF16)], axis=0)
        pv = [_dot(vt_ones, p) for p in ps]
        for n, t in enumerate(tiles):
            m_ref[t] = m_new[n]
            acc_ref[t] = alpha[n] * acc_ref[t] + pv[n]

    def causal_mask(s, rel, t):
        if rel is None or rel + s.shape[0] - 1 <= t * FOX_TILE:
            return s
        key_i = lax.broadcasted_iota(jnp.int32, s.shape, 0) + rel
        qry_i = (lax.broadcasted_iota(jnp.int32, s.shape, 1) & (FOX_TILE - 1)) + t * FOX_TILE
        return jnp.where(key_i <= qry_i, s, -jnp.inf)

    def run(items):
        def issue(n):
            start, size, t0, _ = items[n]
            kx = kx_ref[pl.ds(start, size), :]
            return [scores(t, kx) for t in range(t0, nt)]

        pending = issue(0)
        for n, (start, size, t0, rel) in enumerate(items):
            upcoming = issue(n + 1) if n + 1 < len(items) else None
            vt = vt_ref[:, pl.ds(start, size)]
            tiles = list(range(t0, nt))
            update(tiles, [causal_mask(s, rel, t) for t, s in zip(tiles, pending)], vt)
            pending = upcoming

    def span_items(j, n_blocks):
        base = pl.multiple_of(j * tq, LANES)
        return [(base + off, size, 0, None) for off, size in _key_chunks(n_blocks * tq)]

    def quad_body(jj, carry):
        run(span_items(4 * jj, 4))
        return carry

    lax.fori_loop(0, lax.shift_right_logical(i, 2), quad_body, 0)

    @pl.when((i & 2) == 2)
    def _():
        run(span_items(i & ~3, 2))

    @pl.when((i & 1) == 1)
    def _():
        run(span_items(i - 1, 1))

    base = pl.multiple_of(i * tq, LANES)
    run([(base + off, size, off // FOX_TILE, off) for off, size in _key_chunks(tq)])

    dim_i = lax.broadcasted_iota(jnp.int32, (LANES, FOX_TILE), 0)
    for t in range(nt):
        o = acc_ref[t, 0:LANES, :] / acc_ref[t, LANES:LANES + 1, :]
        own = jnp.where(dim_i < HEAD_DIM, o[:, 0:FOX_TILE], o[:, FOX_TILE:2 * FOX_TILE])
        o_ref[t * FOX_TILE:(t + 1) * FOX_TILE, :] = own.T.astype(BF16)


def _key_chunks(tq):
    chunks, off = [], 0
    while off < tq:
        size = min(2 * LANES, tq - off)
        chunks.append((off, size))
        off += size
    return chunks


def _fox_call(q, kx, vt, *, tq):
    b, lp, _ = q.shape
    nt = tq // FOX_TILE
    return pl.pallas_call(
        functools.partial(_fox_kernel, tq=tq),
        grid=(b, N_PAIRS, lp // tq),
        in_specs=[
            pl.BlockSpec((None, tq, LANES), lambda bi, hp, i: (bi, i, hp)),
            pl.BlockSpec((None, lp, 2 * LANES), lambda bi, hp, i: (bi, 0, hp)),
            pl.BlockSpec((None, LANES, lp), lambda bi, hp, i: (bi, hp, 0)),
        ],
        out_specs=pl.BlockSpec((None, tq, LANES), lambda bi, hp, i: (bi, i, hp)),
        out_shape=jax.ShapeDtypeStruct((b, lp, W_MIX), BF16),
        scratch_shapes=[
            pltpu.VMEM((2 * LANES, 2 * tq), BF16),
            pltpu.VMEM((nt, 1, 2 * FOX_TILE), F32),
            pltpu.VMEM((nt, LANES + FOX_L_ROWS, 2 * FOX_TILE), F32),
        ],
        compiler_params=pltpu.CompilerParams(
            dimension_semantics=("parallel", "parallel", "arbitrary"),
            vmem_limit_bytes=VMEM_LIMIT),
        name="fox_attention",
    )(q, kx, vt)


def _rwkv_kernel(rw_ref, mu_ref, wup_ref, aup_ref, gup_ref, p_ref, o_ref, st_ref, prev_ref):
    cs = RWKV_CHUNK
    n_batch, rows, _ = rw_ref.shape
    n_chunks = rows // cs
    n_levels = int(math.log2(cs))

    @pl.when(pl.program_id(0) == 0)
    def _():
        st_ref[...] = jnp.zeros_like(st_ref)
        prev_ref[...] = jnp.zeros_like(prev_ref)

    prm = p_ref[...]
    w0, a0, k_k, k_a = prm[0:1], prm[1:2], prm[2:3], prm[3:4]
    r_k, gn_w, gn_b = prm[4:5], prm[5:6], prm[6:7]
    tri = _tril_ones(cs, BF16)

    lane = lax.broadcasted_iota(jnp.int32, (cs, LANES), 1)
    rowi = lax.broadcasted_iota(jnp.int32, (cs, LANES), 0)
    lo = lane < HEAD_DIM
    hi = jnp.logical_not(lo)
    col = lane & (HEAD_DIM - 1)
    strict = rowi > col
    incl = rowi >= col
    eye_hi = jnp.where(jnp.logical_and(hi, rowi == col), 1.0, 0.0)
    zeros_blk = jnp.zeros((cs, LANES), BF16)
    row_full = lax.broadcasted_iota(jnp.int32, (cs, RW_COLS), 0)

    def half_sums(t):
        s_lo = jnp.sum(jnp.where(lo, t, 0.0), axis=-1, keepdims=True)
        s_hi = jnp.sum(jnp.where(hi, t, 0.0), axis=-1, keepdims=True)
        return jnp.where(lo, s_lo, s_hi)

    heads = [(bi, h) for bi in range(n_batch) for h in range(N_HEADS)]
    n_heads = len(heads)
    pre = {}

    def shift_and_maps(c):
        for bi in range(n_batch):
            rw = rw_ref[bi, c * cs:(c + 1) * cs, :]
            before = (prev_ref[bi, SUBLANES - 1:SUBLANES, :] if c == 0
                      else rw_ref[bi, c * cs - 1:c * cs, :])
            rw_prev = jnp.where(row_full == 0, before, pltpu.roll(rw, 1, 0))
            z = rw + mu_ref[...] * (rw_prev - rw)
            r_all = z[:, 0:W_MIX]
            kr = z[:, W_MIX:2 * W_MIX]
            v_all = z[:, 2 * W_MIX:3 * W_MIX]
            zw = z[:, Z_W:Z_W + LANES]
            za = z[:, Z_A:Z_A + LANES]
            zg = z[:, Z_G:Z_G + LANES]
            lw = -EXP_M05 * jax.nn.sigmoid(w0 + _dot(jnp.tanh(zw).astype(BF16), wup_ref[...]))
            a_rate = jax.nn.sigmoid(a0 + _dot(za.astype(BF16), aup_ref[...]))
            g_all = _dot(jax.nn.sigmoid(zg).astype(BF16), gup_ref[...])
            k_all = kr * (1.0 + (a_rate - 1.0) * k_a)
            kk = kr * k_k
            kk = jnp.concatenate(
                [kk[:, p * LANES:(p + 1) * LANES]
                 / jnp.maximum(jnp.sqrt(half_sums(jnp.square(kk[:, p * LANES:(p + 1) * LANES]))), 1e-12)
                 for p in range(N_PAIRS)], axis=1)
            pre[c, bi] = dict(r=r_all, k=k_all, v=v_all, g=g_all, lw=lw, kk=kk, b=kk * a_rate)

    def decay_operands(c):
        for bi in range(n_batch):
            d = pre[c, bi]
            lw, kk, b_all, r_all, k_all, v_all = d["lw"], d["kk"], d["b"], d["r"], d["k"], d["v"]
            l1, l2, l3 = _split3_bf16(lw)
            cw = _dot(tri, l1) + _dot(tri, l2) + _dot(tri, l3)
            cw_last = cw[cs - 1:cs, :]
            c_ref = 0.5 * cw_last
            e_cw = jnp.exp(cw)
            e_cwm = jnp.exp(cw - lw)
            e_nref = jnp.exp(-c_ref)
            e_neg = jnp.exp(c_ref - cw)
            e_last = jnp.exp(cw_last - cw)
            d.update(
                v_roll=pltpu.roll(v_all, HEAD_DIM, 1),
                at=-kk * (e_cwm * e_nref), rt=r_all * (e_cw * e_nref),
                bt=b_all * e_neg, kt=k_all * e_neg,
                a0=-kk * e_cwm, r0=r_all * e_cw,
                bh=b_all * e_last, kh=k_all * e_last,
                e_total=jnp.exp(cw_last))

    def grp(c, bi, h, name):
        p = h // 2
        return pre[c, bi][name][:, p * LANES:(p + 1) * LANES]

    def mine(h):
        return hi if h % 2 else lo

    def stack_masked(c, bi, h, top, bot):
        t = jnp.concatenate([grp(c, bi, h, top), grp(c, bi, h, bot)], axis=0)
        m = jnp.concatenate([mine(h), mine(h)], axis=0)
        return jnp.where(m, t, 0.0).astype(BF16)

    m_bot, v_hi, x_v, w = {}, {}, {}, {}

    def score_matrices(c):
        m_all = [_dot_nt(stack_masked(c, bi, h, "at", "rt"),
                         jnp.concatenate([grp(c, bi, h, "bt"), grp(c, bi, h, "kt")], axis=0).astype(BF16))
                 for bi, h in heads]
        m_top = [jnp.where(strict, m[0:cs], 0.0) for m in m_all]
        m_bot[c] = [jnp.where(incl, m[cs:2 * cs], 0.0).astype(BF16) for m in m_all]
        v_hi[c] = [jnp.where(hi, grp(c, bi, h, "v" if h % 2 else "v_roll"), 0.0) for bi, h in heads]
        x_v[c] = [_dot(m_top[i].astype(BF16), jnp.concatenate([zeros_blk, v_hi[c][i].astype(BF16)], axis=0))
                  for i in range(n_heads)]
        w[c] = [jnp.where(lo, m_top[i], eye_hi) for i in range(n_heads)]

    def t_levels(c, count):
        for _ in range(count):
            pw = [_dot(w[c][i][:, 0:HEAD_DIM].astype(BF16), w[c][i].astype(BF16)) for i in range(n_heads)]
            w[c] = [pw[i] + jnp.where(lo, 0.0, w[c][i]) for i in range(n_heads)]

    state = [st_ref[i] for i in range(n_heads)]
    s_in, u_hi, y = {}, {}, {}

    def chain_state_in(c):
        s_in[c] = [_dot_nt(stack_masked(c, bi, h, "a0", "r0"), state[i].astype(BF16))
                   for i, (bi, h) in enumerate(heads)]

    def chain_solve(c):
        t_hi = [jnp.where(lo, 0.0, w[c][i]).astype(BF16) for i in range(n_heads)]
        u_hi[c] = [_dot(t_hi[i], jnp.concatenate([zeros_blk, (s_in[c][i][0:cs] + x_v[c][i]).astype(BF16)], axis=0))
                   for i in range(n_heads)]

    def chain_out(c):
        z32 = [jnp.concatenate([u_hi[c][i], v_hi[c][i]], axis=0) for i in range(n_heads)]
        for i, (bi, h) in enumerate(heads):
            state[i] = (state[i] * grp(c, bi, h, "e_total")
                        + _dot(z32[i].T.astype(BF16), stack_masked(c, bi, h, "bh", "kh")))
        y[c] = [s_in[c][i][cs:2 * cs] + _dot(m_bot[c][i], z32[i].astype(BF16)) for i in range(n_heads)]

    def finish(c):
        for bi in range(n_batch):
            d = pre[c, bi]
            for p in range(N_PAIRS):
                i_even = bi * N_HEADS + 2 * p
                sl = slice(p * LANES, (p + 1) * LANES)
                yp = pltpu.roll(y[c][i_even], HEAD_DIM, 1) + y[c][i_even + 1]
                mean = half_sums(yp) * (1.0 / HEAD_DIM)
                dv = yp - mean
                var = half_sums(dv * dv) * (1.0 / HEAD_DIM)
                yn = dv * lax.rsqrt(var + GN_EPS) * gn_w[:, sl] + gn_b[:, sl]
                bonus = half_sums(d["r"][:, sl] * d["k"][:, sl] * r_k[:, sl]) * d["v"][:, sl]
                o_ref[bi, c * cs:(c + 1) * cs, sl] = ((yn + bonus) * d["g"][:, sl]).astype(BF16)

    def on(c, fn, *args):
        if 0 <= c < n_chunks:
            fn(c, *args)

    for s in range(n_chunks + 3):
        on(s - 1, t_levels, 1)
        on(s - 2, chain_state_in)
        on(s, shift_and_maps)
        on(s - 1, t_levels, 2)
        on(s - 2, chain_solve)
        on(s, decay_operands)
        on(s - 1, t_levels, 2)
        on(s - 2, chain_out)
        on(s - 3, finish)
        on(s - 1, t_levels, n_levels - 5)
        on(s, score_matrices)
    for bi in range(n_batch):
        prev_ref[bi] = rw_ref[bi, rows - SUBLANES:rows, :]
    for i in range(n_heads):
        st_ref[i] = state[i]


def _rwkv_call(rw, mu, w_up, a_up, g_up, prm, *, layer):
    b, lp, _ = rw.shape
    rows = RWKV_CHUNK * next(n for n in RWKV_CHUNKS_PER_STEP if lp % (RWKV_CHUNK * n) == 0)
    return pl.pallas_call(
        _rwkv_kernel,
        grid=(lp // rows,),
        in_specs=[
            pl.BlockSpec((b, rows, RW_COLS), lambda c: (0, c, 0)),
            _const_spec((1, RW_COLS), layer),
            _const_spec((LANES, W_MIX), layer),
            _const_spec((LANES, W_MIX), layer),
            _const_spec((LANES, W_MIX), layer),
            _const_spec((SUBLANES, W_MIX), layer),
        ],
        out_specs=pl.BlockSpec((b, rows, W_MIX), lambda c: (0, c, 0)),
        out_shape=jax.ShapeDtypeStruct((b, lp, W_MIX), BF16),
        scratch_shapes=[
            pltpu.VMEM((b * N_HEADS, 2 * RWKV_CHUNK, LANES), F32),
            pltpu.VMEM((b, SUBLANES, RW_COLS), F32),
        ],
        compiler_params=pltpu.CompilerParams(
            dimension_semantics=("arbitrary",), vmem_limit_bytes=VMEM_LIMIT),
        name="rwkv7",
    )(rw, mu, w_up, a_up, g_up, prm)


def _merge_kernel(h_ref, g_ref, of_ref, or_ref, wg_ref, gb_ref, wbf_ref, wbr_ref, wo_ref, out_ref):
    x = h_ref[...]
    hn = _rms_norm_rows(x, g_ref[...]).astype(BF16)
    gates = jax.nn.sigmoid(_dot(hn, wg_ref[...]) + gb_ref[...])
    merged = (gates[:, 0:D_MODEL] * _dot(of_ref[...], wbf_ref[...])
              + gates[:, D_MODEL:2 * D_MODEL] * _dot(or_ref[...], wbr_ref[...]))
    out_ref[...] = x + _dot(merged.astype(BF16), wo_ref[...])


def _merge_call(h, gain, o_fox, o_rwkv, w_g, g_bias, w_bf, w_br, w_o, *, layer, tm):
    b, lp, _ = h.shape
    row = lambda n: pl.BlockSpec((None, tm, n), lambda bi, i: (bi, i, 0))
    return pl.pallas_call(
        _merge_kernel,
        grid=(b, lp // tm),
        in_specs=[
            row(D_MODEL), _const_spec((1, D_MODEL), layer), row(W_MIX), row(W_MIX),
            _const_spec((D_MODEL, 2 * D_MODEL), layer), _const_spec((1, 2 * D_MODEL), layer),
            _const_spec((W_MIX, D_MODEL), layer), _const_spec((W_MIX, D_MODEL), layer),
            _const_spec((D_MODEL, D_MODEL), layer),
        ],
        out_specs=row(D_MODEL),
        out_shape=jax.ShapeDtypeStruct(h.shape, F32),
        compiler_params=pltpu.CompilerParams(
            dimension_semantics=("parallel", "parallel"), vmem_limit_bytes=VMEM_LIMIT),
        name="merge",
    )(h, gain, o_fox, o_rwkv, w_g, g_bias, w_bf, w_br, w_o)


FF_CHUNK = D_FF // 2


def _ffn_kernel(h_ref, g_ref, wup_ref, wc_ref, wd_ref, out_ref, carry_ref, *, tm):
    @pl.when(pl.program_id(1) == 0)
    def _():
        carry_ref[...] = jnp.zeros_like(carry_ref)

    x = h_ref[...]
    hn = _rms_norm_rows(x, g_ref[...]).astype(BF16)
    wc = wc_ref[...]
    row = lax.broadcasted_iota(jnp.int32, (tm, FF_CHUNK), 0)

    def conv(u, col):
        prev = carry_ref[:, col:col + FF_CHUNK]
        p1 = prev[SUBLANES - 1:SUBLANES, :]
        p2 = prev[SUBLANES - 2:SUBLANES - 1, :]
        u1 = jnp.where(row == 0, p1, pltpu.roll(u, 1, 0))
        u2 = jnp.where(row == 0, p2, jnp.where(row == 1, p1, pltpu.roll(u, 2, 0)))
        carry_ref[:, col:col + FF_CHUNK] = u[tm - SUBLANES:tm, :]
        w = wc[:, col:col + FF_CHUNK]
        return w[0:1] * u2 + w[1:2] * u1 + w[2:3] * u

    acc = x
    for c in range(D_FF // FF_CHUNK):
        cg = c * FF_CHUNK
        cv = D_FF + c * FF_CHUNK
        gate = conv(_dot(hn, wup_ref[:, cg:cg + FF_CHUNK]), cg)
        val = conv(_dot(hn, wup_ref[:, cv:cv + FF_CHUNK]), cv)
        act = (gate * jax.nn.sigmoid(gate) * val).astype(BF16)
        acc = acc + _dot(act, wd_ref[cg:cg + FF_CHUNK, :])
    out_ref[...] = acc


def _ffn_call(h, gain, w_up, w_conv, w_down, *, layer, tm):
    b, lp, _ = h.shape
    row = pl.BlockSpec((None, tm, D_MODEL), lambda bi, i: (bi, i, 0))
    return pl.pallas_call(
        functools.partial(_ffn_kernel, tm=tm),
        grid=(b, lp // tm),
        in_specs=[
            row, _const_spec((1, D_MODEL), layer),
            _const_spec((D_MODEL, 2 * D_FF), layer), _const_spec((CONV_W, 2 * D_FF), layer),
            _const_spec((D_FF, D_MODEL), layer),
        ],
        out_specs=row,
        out_shape=jax.ShapeDtypeStruct(h.shape, F32),
        scratch_shapes=[pltpu.VMEM((SUBLANES, 2 * D_FF), F32)],
        compiler_params=pltpu.CompilerParams(
            dimension_semantics=("parallel", "arbitrary"), vmem_limit_bytes=VMEM_LIMIT),
        name="conv_ffn",
    )(h, gain, w_up, w_conv, w_down)


def _row_tile(lp):
    for t in (640, 512, 384, 256, 128):
        if lp % t == 0:
            return t
    raise ValueError(f"padded length {lp} has no supported row tile")


def _pad_cols(w, n):
    return jnp.pad(w, [(0, 0)] * (w.ndim - 1) + [(0, n - w.shape[-1])])


def _prep_in_weights(w_in, f_bias, mu):
    o_f = 3 * W_MIX
    o_rw = o_f + N_HEADS
    n_rw = 3 * W_MIX + LORA_W + LORA_A + LORA_G
    o_g = o_rw + n_rw
    w_bf = w_in.astype(BF16)

    def rw_layout(t):
        return jnp.concatenate([
            t[..., 0:3 * W_MIX],
            _pad_cols(t[..., 3 * W_MIX:3 * W_MIX + LORA_W], LANES),
            _pad_cols(t[..., 3 * W_MIX + LORA_W:3 * W_MIX + LORA_W + LORA_A], LANES),
            t[..., 3 * W_MIX + LORA_W + LORA_A:],
        ], axis=-1)

    w_a = jnp.concatenate([
        w_bf[..., 0:2 * W_MIX],
        _pad_cols(w_bf[..., o_f:o_rw], LANES),
        rw_layout(w_bf[..., o_rw:o_g]),
    ], axis=-1)
    w_vt = jnp.swapaxes(w_bf[..., 2 * W_MIX:3 * W_MIX], -1, -2)
    w_g = w_bf[..., o_g:]
    fb = _pad_cols(f_bias, LANES)[:, None, :]
    return w_a, w_vt, w_g, fb, rw_layout(mu)[:, None, :]


def kernel(x, meta_tokens, norm_mix, norm_ffn, w_in, fox_q_norm, fox_k_norm, fox_f_bias,
           rwkv_shift_mu, rwkv_w0, rwkv_w_up, rwkv_a0, rwkv_a_up, rwkv_g_up, rwkv_k_k,
           rwkv_k_a, rwkv_r_k, rwkv_gn_w, rwkv_gn_b, w_branch_fox, w_branch_rwkv,
           gate_bias, w_out, ffn_up, ffn_conv, ffn_down):
    b, s, _ = x.shape
    depth = w_in.shape[0]
    length = N_META + s
    lp = -(-length // LANES) * LANES
    tm = _row_tile(lp)

    meta = jnp.broadcast_to(meta_tokens.astype(x.dtype)[None], (b, N_META, D_MODEL))
    h = jnp.concatenate([meta, x], axis=1)
    h = jnp.pad(h, ((0, 0), (0, lp - length), (0, 0)))

    w_a, w_vt, w_g, f_b, mu = _prep_in_weights(w_in, fox_f_bias, rwkv_shift_mu)
    qk_gain = jnp.stack([jnp.tile(fox_q_norm, (1, N_HEADS)), jnp.tile(fox_k_norm, (1, N_HEADS))], axis=1)
    bd = (jnp.kron(jnp.eye(N_HEADS, dtype=F32), jnp.ones((HEAD_DIM, HEAD_DIM), F32)) / HEAD_DIM).astype(BF16)
    pad_rows = lambda w: jnp.pad(w, ((0, 0), (0, LANES - w.shape[1]), (0, 0))).astype(BF16)
    w_up_l, a_up_l, g_up_l = pad_rows(rwkv_w_up), pad_rows(rwkv_a_up), rwkv_g_up.astype(BF16)
    prm = jnp.stack([rwkv_w0, rwkv_a0, rwkv_k_k, rwkv_k_a, rwkv_r_k.reshape(depth, W_MIX),
                     rwkv_gn_w, rwkv_gn_b, jnp.zeros_like(rwkv_w0)], axis=1)
    w_bf, w_br, w_o = w_branch_fox.astype(BF16), w_branch_rwkv.astype(BF16), w_out.astype(BF16)
    f_up, f_down = ffn_up.astype(BF16), ffn_down.astype(BF16)
    g_mix, g_ffn, g_bias = norm_mix[:, None, :], norm_ffn[:, None, :], gate_bias[:, None, :]

    for i in range(depth):
        q, kx, vt, rw = _in_call(h, g_mix, w_a, w_vt, qk_gain, bd, f_b, layer=i, tm=tm)
        o_fox = _fox_call(q, kx, vt, tq=tm)
        o_rwkv = _rwkv_call(rw, mu, w_up_l, a_up_l, g_up_l, prm, layer=i)
        h = _merge_call(h, g_mix, o_fox, o_rwkv, w_g, g_bias, w_bf, w_br, w_o, layer=i, tm=tm)
        h = _ffn_call(h, g_ffn, f_up, ffn_conv, f_down, layer=i, tm=tm)
    return h[:, N_META:N_META + s]
```

```python
import functools
import math

import jax
import jax.numpy as jnp
from jax import lax
from jax.experimental import pallas as pl
from jax.experimental.pallas import tpu as pltpu

D_MODEL = 1024
N_META = 16
HEAD_DIM = 64
N_HEADS = 8
W_MIX = N_HEADS * HEAD_DIM
LORA_W = 64
LORA_A = 64
LORA_G = 128
D_FF = 2816
CONV_W = 3
NORM_EPS = 1e-6
GN_EPS = HEAD_DIM * 1e-5
DEPTH = 4

LANES = 128
SUBLANES = 8
VMEM_LIMIT = 56 * 1024 * 1024

C_Q = 0
C_K = C_Q + W_MIX
C_F = C_K + W_MIX
C_RW = C_F + LANES
RW_COLS = 3 * W_MIX + 3 * LANES
N_A = C_RW + RW_COLS
Z_W = 3 * W_MIX
Z_A = Z_W + LANES
Z_G = Z_A + LANES

RWKV_CHUNK = 64
RWKV_CHUNKS_PER_STEP = (5, 4, 2, 1)
EXP_M05 = math.exp(-0.5)
LOG2E = math.log2(math.e)
N_PAIRS = W_MIX // LANES
FOX_TILE = LANES
GATE_TERMS = 3
FOX_L_ROWS = 16

BF16 = jnp.bfloat16
F32 = jnp.float32


def _dot(a, b):
    return jnp.dot(a, b, preferred_element_type=F32)


def _dot_nt(a, b):
    return lax.dot_general(a, b, (((1,), (1,)), ((), ())), preferred_element_type=F32)


def _rms_norm_rows(x, gain):
    ms = jnp.mean(x * x, axis=-1, keepdims=True)
    return x * lax.rsqrt(ms + NORM_EPS) * gain


def _split3_bf16(x):
    h1 = x.astype(BF16)
    r1 = x - h1.astype(F32)
    h2 = r1.astype(BF16)
    h3 = (r1 - h2.astype(F32)).astype(BF16)
    return h1, h2, h3


def _tril_ones(n, dtype):
    r = lax.broadcasted_iota(jnp.int32, (n, n), 0)
    c = lax.broadcasted_iota(jnp.int32, (n, n), 1)
    return (r >= c).astype(dtype)


def _const_spec(shape, layer=None):
    if layer is None:
        return pl.BlockSpec(shape, lambda *_: (0,) * len(shape), pipeline_mode=pl.Buffered(1))
    return pl.BlockSpec((None,) + tuple(shape), lambda *_: (layer,) + (0,) * len(shape),
                        pipeline_mode=pl.Buffered(1))


def _in_kernel(h_ref, g_ref, w_ref, wvt_ref, qkg_ref, bd_ref, fb_ref, sel_ref,
               q_ref, kx_ref, vt_ref, rw_ref, carry_ref, *, tm):
    @pl.when(pl.program_id(1) == 0)
    def _():
        carry_ref[...] = jnp.zeros_like(carry_ref)

    hn = _rms_norm_rows(h_ref[...], g_ref[...]).astype(BF16)
    bd = bd_ref[...]

    def head_norm(t, gain):
        ms = _dot((t * t).astype(BF16), bd)
        return t * lax.rsqrt(ms + NORM_EPS) * gain

    q = _dot(hn, w_ref[:, C_Q:C_Q + W_MIX])
    q_ref[...] = (head_norm(q, qkg_ref[0:1, :]) * (HEAD_DIM ** -0.5 * LOG2E)).astype(BF16)
    k = head_norm(_dot(hn, w_ref[:, C_K:C_K + W_MIX]), qkg_ref[1:2, :]).astype(BF16)
    for p in range(N_PAIRS):
        kx_ref[:, 2 * p * LANES:(2 * p + 1) * LANES] = k[:, p * LANES:(p + 1) * LANES]
    vt_ref[...] = _dot_nt(wvt_ref[...], hn).astype(BF16)
    rw_ref[...] = _dot(hn, w_ref[:, C_RW:C_RW + RW_COLS])

    f = _dot(hn, w_ref[:, C_F:C_F + LANES]) + fb_ref[...]
    log_f = jnp.minimum(f, 0.0) - jnp.log1p(jnp.exp(-jnp.abs(f)))
    tri = _tril_ones(LANES, BF16)
    carry = carry_ref[0:1, :]
    for blk in range(tm // LANES):
        rows = slice(blk * LANES, (blk + 1) * LANES)
        h1, h2, h3 = _split3_bf16(log_f[rows])
        c = _dot(tri, h1) + _dot(tri, h2) + _dot(tri, h3) + carry
        b1, b2, b3 = _split3_bf16(c * (-LOG2E))
        bias = (_dot(b1, sel_ref[0]) + _dot(b2, sel_ref[1]) + _dot(b3, sel_ref[2])).astype(BF16)
        for p in range(N_PAIRS):
            kx_ref[rows, (2 * p + 1) * LANES:(2 * p + 2) * LANES] = bias
        carry = c[LANES - 1:LANES, :]
    carry_ref[...] = jnp.broadcast_to(carry, carry_ref.shape)


def _gate_selectors():
    sel = [[[0.0] * LANES for _ in range(LANES)] for _ in range(GATE_TERMS)]
    for h in range(N_HEADS):
        for term in range(GATE_TERMS):
            sel[term][h][GATE_TERMS * h + term] = 1.0
    return jnp.asarray(sel, dtype=BF16)


def _in_call(h, gain, w_a, w_vt, qk_gain, bd, f_bias, *, layer, tm):
    b, lp, _ = h.shape
    grid = (b, lp // tm)
    row = lambda n: pl.BlockSpec((None, tm, n), lambda bi, i: (bi, i, 0))
    return pl.pallas_call(
        functools.partial(_in_kernel, tm=tm),
        grid=grid,
        in_specs=[
            row(D_MODEL),
            _const_spec((1, D_MODEL), layer),
            _const_spec((D_MODEL, N_A), layer),
            _const_spec((W_MIX, D_MODEL), layer),
            _const_spec((2, W_MIX), layer),
            _const_spec((W_MIX, W_MIX)),
            _const_spec((1, LANES), layer),
            _const_spec((GATE_TERMS, LANES, LANES)),
        ],
        out_specs=[
            row(W_MIX), row(2 * W_MIX),
            pl.BlockSpec((None, W_MIX, tm), lambda bi, i: (bi, 0, i)),
            row(RW_COLS),
        ],
        out_shape=[
            jax.ShapeDtypeStruct((b, lp, W_MIX), BF16),
            jax.ShapeDtypeStruct((b, lp, 2 * W_MIX), BF16),
            jax.ShapeDtypeStruct((b, W_MIX, lp), BF16),
            jax.ShapeDtypeStruct((b, lp, RW_COLS), F32),
        ],
        scratch_shapes=[pltpu.VMEM((SUBLANES, LANES), F32)],
        compiler_params=pltpu.CompilerParams(
            dimension_semantics=("parallel", "arbitrary"), vmem_limit_bytes=VMEM_LIMIT),
        name="in_proj",
    )(h, gain, w_a, w_vt, qk_gain, bd, f_bias, _gate_selectors())


def _fox_kernel(q_ref, kx_ref, vt_ref, o_ref, qx_ref, m_ref, acc_ref, *, tq):
    i = pl.program_id(2)
    nt = tq // FOX_TILE
    q_t = q_ref[...].astype(F32).T
    row = lax.broadcasted_iota(jnp.int32, (LANES, FOX_TILE), 0)
    first = row < HEAD_DIM
    g0 = 2 * GATE_TERMS * pl.program_id(1)
    gate_a = jnp.where(jnp.logical_and(row >= g0, row < g0 + GATE_TERMS), 1.0, 0.0).astype(BF16)
    gate_b = jnp.where(jnp.logical_and(row >= g0 + GATE_TERMS, row < g0 + 2 * GATE_TERMS),
                       1.0, 0.0).astype(BF16)
    for t in range(nt):
        qt = q_t[:, t * FOX_TILE:(t + 1) * FOX_TILE]
        c0 = 2 * t * FOX_TILE
        qx_ref[0:LANES, c0:c0 + FOX_TILE] = jnp.where(first, qt, 0.0).astype(BF16)
        qx_ref[LANES:2 * LANES, c0:c0 + FOX_TILE] = gate_a
        qx_ref[0:LANES, c0 + FOX_TILE:c0 + 2 * FOX_TILE] = jnp.where(first, 0.0, qt).astype(BF16)
        qx_ref[LANES:2 * LANES, c0 + FOX_TILE:c0 + 2 * FOX_TILE] = gate_b
    m_ref[...] = jnp.full_like(m_ref, -jnp.inf)
    acc_ref[...] = jnp.zeros_like(acc_ref)

    def scores(t, kx):
        return _dot(kx, qx_ref[:, 2 * t * FOX_TILE:2 * (t + 1) * FOX_TILE])

    def update(tiles, ss, vt):
        m_prev = [m_ref[t] for t in tiles]
        m_new = [jnp.maximum(mp, jnp.max(s, axis=0, keepdims=True)) for mp, s in zip(m_prev, ss)]
        alpha = [jnp.exp2(mp - mn) for mp, mn in zip(m_prev, m_new)]
        ps = [jnp.exp2((s - mn).astype(BF16)) for s, mn in zip(ss, m_new)]
        vt_ones = jnp.concatenate([vt, jnp.ones((FOX_L_ROWS, vt.shape[1]), BF16)], axis=0)
        pv = [_dot(vt_ones, p) for p in ps]
        for n, t in enumerate(tiles):
            m_ref[t] = m_new[n]
            acc_ref[t] = alpha[n] * acc_ref[t] + pv[n]

    def causal_mask(s, rel, t):
        if rel is None or rel + s.shape[0] - 1 <= t * FOX_TILE:
            return s
        key_i = lax.broadcasted_iota(jnp.int32, s.shape, 0) + rel
        qry_i = (lax.broadcasted_iota(jnp.int32, s.shape, 1) & (FOX_TILE - 1)) + t * FOX_TILE
        return jnp.where(key_i <= qry_i, s, -jnp.inf)

    def run(items):
        def issue(n):
            start, size, t0, _ = items[n]
            kx = kx_ref[pl.ds(start, size), :]
            return [scores(t, kx) for t in range(t0, nt)]

        pending = issue(0)
        for n, (start, size, t0, rel) in enumerate(items):
            upcoming = issue(n + 1) if n + 1 < len(items) else None
            vt = vt_ref[:, pl.ds(start, size)]
            tiles = list(range(t0, nt))
            update(tiles, [causal_mask(s, rel, t) for t, s in zip(tiles, pending)], vt)
            pending = upcoming

    def span_items(j, n_blocks):
        base = pl.multiple_of(j * tq, LANES)
        return [(base + off, size, 0, None) for off, size in _key_chunks(n_blocks * tq)]

    def quad_body(jj, carry):
        run(span_items(4 * jj, 4))
        return carry

    lax.fori_loop(0, lax.shift_right_logical(i, 2), quad_body, 0)

    @pl.when((i & 2) == 2)
    def _():
        run(span_items(i & ~3, 2))

    @pl.when((i & 1) == 1)
    def _():
        run(span_items(i - 1, 1))

    base = pl.multiple_of(i * tq, LANES)
    run([(base + off, size, off // FOX_TILE, off) for off, size in _key_chunks(tq)])

    dim_i = lax.broadcasted_iota(jnp.int32, (LANES, FOX_TILE), 0)
    for t in range(nt):
        o = acc_ref[t, 0:LANES, :] / acc_ref[t, LANES:LANES + 1, :]
        own = jnp.where(dim_i < HEAD_DIM, o[:, 0:FOX_TILE], o[:, FOX_TILE:2 * FOX_TILE])
        o_ref[t * FOX_TILE:(t + 1) * FOX_TILE, :] = own.T.astype(BF16)


def _key_chunks(tq):
    chunks, off = [], 0
    while off < tq:
        size = min(2 * LANES, tq - off)
        chunks.append((off, size))
        off += size
    return chunks


def _fox_call(q, kx, vt, *, tq):
    b, lp, _ = q.shape
    nt = tq // FOX_TILE
    return pl.pallas_call(
        functools.partial(_fox_kernel, tq=tq),
        grid=(b, N_PAIRS, lp // tq),
        in_specs=[
            pl.BlockSpec((None, tq, LANES), lambda bi, hp, i: (bi, i, hp)),
            pl.BlockSpec((None, lp, 2 * LANES), lambda bi, hp, i: (bi, 0, hp)),
            pl.BlockSpec((None, LANES, lp), lambda bi, hp, i: (bi, hp, 0)),
        ],
        out_specs=pl.BlockSpec((None, tq, LANES), lambda bi, hp, i: (bi, i, hp)),
        out_shape=jax.ShapeDtypeStruct((b, lp, W_MIX), BF16),
        scratch_shapes=[
            pltpu.VMEM((2 * LANES, 2 * tq), BF16),
            pltpu.VMEM((nt, 1, 2 * FOX_TILE), F32),
            pltpu.VMEM((nt, LANES + FOX_L_ROWS, 2 * FOX_TILE), F32),
        ],
        compiler_params=pltpu.CompilerParams(
            dimension_semantics=("parallel", "parallel", "arbitrary"),
            vmem_limit_bytes=VMEM_LIMIT),
        name="fox_attention",
    )(q, kx, vt)


def _rwkv_kernel(rw_ref, mu_ref, wup_ref, aup_ref, gup_ref, p_ref, o_ref, st_ref, prev_ref):
    cs = RWKV_CHUNK
    n_batch, rows, _ = rw_ref.shape
    n_chunks = rows // cs
    n_levels = int(math.log2(cs))

    @pl.when(pl.program_id(0) == 0)
    def _():
        st_ref[...] = jnp.zeros_like(st_ref)
        prev_ref[...] = jnp.zeros_like(prev_ref)

    prm = p_ref[...]
    w0, a0, k_k, k_a = prm[0:1], prm[1:2], prm[2:3], prm[3:4]
    r_k, gn_w, gn_b = prm[4:5], prm[5:6], prm[6:7]
    tri = _tril_ones(cs, BF16)

    lane = lax.broadcasted_iota(jnp.int32, (cs, LANES), 1)
    rowi = lax.broadcasted_iota(jnp.int32, (cs, LANES), 0)
    lo = lane < HEAD_DIM
    hi = jnp.logical_not(lo)
    col = lane & (HEAD_DIM - 1)
    strict = rowi > col
    incl = rowi >= col
    eye_hi = jnp.where(jnp.logical_and(hi, rowi == col), 1.0, 0.0)
    zeros_blk = jnp.zeros((cs, LANES), BF16)
    row_full = lax.broadcasted_iota(jnp.int32, (cs, RW_COLS), 0)

    def half_sums(t):
        s_lo = jnp.sum(jnp.where(lo, t, 0.0), axis=-1, keepdims=True)
        s_hi = jnp.sum(jnp.where(hi, t, 0.0), axis=-1, keepdims=True)
        return jnp.where(lo, s_lo, s_hi)

    heads = [(bi, h) for bi in range(n_batch) for h in range(N_HEADS)]
    n_heads = len(heads)
    pre = {}

    def shift_and_maps(c):
        for bi in range(n_batch):
            rw = rw_ref[bi, c * cs:(c + 1) * cs, :]
            before = (prev_ref[bi, SUBLANES - 1:SUBLANES, :] if c == 0
                      else rw_ref[bi, c * cs - 1:c * cs, :])
            rw_prev = jnp.where(row_full == 0, before, pltpu.roll(rw, 1, 0))
            z = rw + mu_ref[...] * (rw_prev - rw)
            r_all = z[:, 0:W_MIX]
            kr = z[:, W_MIX:2 * W_MIX]
            v_all = z[:, 2 * W_MIX:3 * W_MIX]
            zw = z[:, Z_W:Z_W + LANES]
            za = z[:, Z_A:Z_A + LANES]
            zg = z[:, Z_G:Z_G + LANES]
            lw = -EXP_M05 * jax.nn.sigmoid(w0 + _dot(jnp.tanh(zw).astype(BF16), wup_ref[...]))
            a_rate = jax.nn.sigmoid(a0 + _dot(za.astype(BF16), aup_ref[...]))
            g_all = _dot(jax.nn.sigmoid(zg).astype(BF16), gup_ref[...])
            k_all = kr * (1.0 + (a_rate - 1.0) * k_a)
            kk = kr * k_k
            kk = jnp.concatenate(
                [kk[:, p * LANES:(p + 1) * LANES]
                 / jnp.maximum(jnp.sqrt(half_sums(jnp.square(kk[:, p * LANES:(p + 1) * LANES]))), 1e-12)
                 for p in range(N_PAIRS)], axis=1)
            pre[c, bi] = dict(r=r_all, k=k_all, v=v_all, g=g_all, lw=lw, kk=kk, b=kk * a_rate)

    def decay_operands(c):
        for bi in range(n_batch):
            d = pre[c, bi]
            lw, kk, b_all, r_all, k_all, v_all = d["lw"], d["kk"], d["b"], d["r"], d["k"], d["v"]
            l1, l2, l3 = _split3_bf16(lw)
            cw = _dot(tri, l1) + _dot(tri, l2) + _dot(tri, l3)
            cw_last = cw[cs - 1:cs, :]
            c_ref = 0.5 * cw_last
            e_cw = jnp.exp(cw)
            e_cwm = jnp.exp(cw - lw)
            e_nref = jnp.exp(-c_ref)
            e_neg = jnp.exp(c_ref - cw)
            e_last = jnp.exp(cw_last - cw)
            d.update(
                v_roll=pltpu.roll(v_all, HEAD_DIM, 1),
                at=-kk * (e_cwm * e_nref), rt=r_all * (e_cw * e_nref),
                bt=b_all * e_neg, kt=k_all * e_neg,
                a0=-kk * e_cwm, r0=r_all * e_cw,
                bh=b_all * e_last, kh=k_all * e_last,
                e_total=jnp.exp(cw_last))

    def grp(c, bi, h, name):
        p = h // 2
        return pre[c, bi][name][:, p * LANES:(p + 1) * LANES]

    def mine(h):
        return hi if h % 2 else lo

    def stack_masked(c, bi, h, top, bot):
        t = jnp.concatenate([grp(c, bi, h, top), grp(c, bi, h, bot)], axis=0)
        m = jnp.concatenate([mine(h), mine(h)], axis=0)
        return jnp.where(m, t, 0.0).astype(BF16)

    m_bot, v_hi, x_v, w = {}, {}, {}, {}

    def score_matrices(c):
        m_all = [_dot_nt(stack_masked(c, bi, h, "at", "rt"),
                         jnp.concatenate([grp(c, bi, h, "bt"), grp(c, bi, h, "kt")], axis=0).astype(BF16))
                 for bi, h in heads]
        m_top = [jnp.where(strict, m[0:cs], 0.0) for m in m_all]
        m_bot[c] = [jnp.where(incl, m[cs:2 * cs], 0.0).astype(BF16) for m in m_all]
        v_hi[c] = [jnp.where(hi, grp(c, bi, h, "v" if h % 2 else "v_roll"), 0.0) for bi, h in heads]
        x_v[c] = [_dot(m_top[i].astype(BF16), jnp.concatenate([zeros_blk, v_hi[c][i].astype(BF16)], axis=0))
                  for i in range(n_heads)]
        w[c] = [jnp.where(lo, m_top[i], eye_hi) for i in range(n_heads)]

    def t_levels(c, count):
        for _ in range(count):
            pw = [_dot(w[c][i][:, 0:HEAD_DIM].astype(BF16), w[c][i].astype(BF16)) for i in range(n_heads)]
            w[c] = [pw[i] + jnp.where(lo, 0.0, w[c][i]) for i in range(n_heads)]

    state = [st_ref[i] for i in range(n_heads)]
    s_in, u_hi, y = {}, {}, {}

    def chain_state_in(c):
        s_in[c] = [_dot_nt(stack_masked(c, bi, h, "a0", "r0"), state[i].astype(BF16))
                   for i, (bi, h) in enumerate(heads)]

    def chain_solve(c):
        t_hi = [jnp.where(lo, 0.0, w[c][i]).astype(BF16) for i in range(n_heads)]
        u_hi[c] = [_dot(t_hi[i], jnp.concatenate([zeros_blk, (s_in[c][i][0:cs] + x_v[c][i]).astype(BF16)], axis=0))
                   for i in range(n_heads)]

    def chain_out(c):
        z32 = [jnp.concatenate([u_hi[c][i], v_hi[c][i]], axis=0) for i in range(n_heads)]
        for i, (bi, h) in enumerate(heads):
            state[i] = (state[i] * grp(c, bi, h, "e_total")
                        + _dot(z32[i].T.astype(BF16), stack_masked(c, bi, h, "bh", "kh")))
        y[c] = [s_in[c][i][cs:2 * cs] + _dot(m_bot[c][i], z32[i].astype(BF16)) for i in range(n_heads)]

    def finish(c):
        for bi in range(n_batch):
            d = pre[c, bi]
            for p in range(N_PAIRS):
                i_even = bi * N_HEADS + 2 * p
                sl = slice(p * LANES, (p + 1) * LANES)
                yp = pltpu.roll(y[c][i_even], HEAD_DIM, 1) + y[c][i_even + 1]
                mean = half_sums(yp) * (1.0 / HEAD_DIM)
                dv = yp - mean
                var = half_sums(dv * dv) * (1.0 / HEAD_DIM)
                yn = dv * lax.rsqrt(var + GN_EPS) * gn_w[:, sl] + gn_b[:, sl]
                bonus = half_sums(d["r"][:, sl] * d["k"][:, sl] * r_k[:, sl]) * d["v"][:, sl]
                o_ref[bi, c * cs:(c + 1) * cs, sl] = ((yn + bonus) * d["g"][:, sl]).astype(BF16)

    def on(c, fn, *args):
        if 0 <= c < n_chunks:
            fn(c, *args)

    for s in range(n_chunks + 3):
        on(s - 1, t_levels, 1)
        on(s - 2, chain_state_in)
        on(s, shift_and_maps)
        on(s - 1, t_levels, 2)
        on(s - 2, chain_solve)
        on(s, decay_operands)
        on(s - 1, t_levels, 2)
        on(s - 2, chain_out)
        on(s - 3, finish)
        on(s - 1, t_levels, n_levels - 5)
        on(s, score_matrices)
    for bi in range(n_batch):
        prev_ref[bi] = rw_ref[bi, rows - SUBLANES:rows, :]
    for i in range(n_heads):
        st_ref[i] = state[i]


def _rwkv_call(rw, mu, w_up, a_up, g_up, prm, *, layer):
    b, lp, _ = rw.shape
    rows = RWKV_CHUNK * next(n for n in RWKV_CHUNKS_PER_STEP if lp % (RWKV_CHUNK * n) == 0)
    return pl.pallas_call(
        _rwkv_kernel,
        grid=(lp // rows,),
        in_specs=[
            pl.BlockSpec((b, rows, RW_COLS), lambda c: (0, c, 0)),
            _const_spec((1, RW_COLS), layer),
            _const_spec((LANES, W_MIX), layer),
            _const_spec((LANES, W_MIX), layer),
            _const_spec((LANES, W_MIX), layer),
            _const_spec((SUBLANES, W_MIX), layer),
        ],
        out_specs=pl.BlockSpec((b, rows, W_MIX), lambda c: (0, c, 0)),
        out_shape=jax.ShapeDtypeStruct((b, lp, W_MIX), BF16),
        scratch_shapes=[
            pltpu.VMEM((b * N_HEADS, 2 * RWKV_CHUNK, LANES), F32),
            pltpu.VMEM((b, SUBLANES, RW_COLS), F32),
        ],
        compiler_params=pltpu.CompilerParams(
            dimension_semantics=("arbitrary",), vmem_limit_bytes=VMEM_LIMIT),
        name="rwkv7",
    )(rw, mu, w_up, a_up, g_up, prm)


def _merge_kernel(h_ref, g_ref, of_ref, or_ref, wg_ref, gb_ref, wbf_ref, wbr_ref, wo_ref, out_ref):
    x = h_ref[...]
    hn = _rms_norm_rows(x, g_ref[...]).astype(BF16)
    gates = jax.nn.sigmoid(_dot(hn, wg_ref[...]) + gb_ref[...])
    merged = (gates[:, 0:D_MODEL] * _dot(of_ref[...], wbf_ref[...])
              + gates[:, D_MODEL:2 * D_MODEL] * _dot(or_ref[...], wbr_ref[...]))
    out_ref[...] = x + _dot(merged.astype(BF16), wo_ref[...])


def _merge_call(h, gain, o_fox, o_rwkv, w_g, g_bias, w_bf, w_br, w_o, *, layer, tm):
    b, lp, _ = h.shape
    row = lambda n: pl.BlockSpec((None, tm, n), lambda bi, i: (bi, i, 0))
    return pl.pallas_call(
        _merge_kernel,
        grid=(b, lp // tm),
        in_specs=[
            row(D_MODEL), _const_spec((1, D_MODEL), layer), row(W_MIX), row(W_MIX),
            _const_spec((D_MODEL, 2 * D_MODEL), layer), _const_spec((1, 2 * D_MODEL), layer),
            _const_spec((W_MIX, D_MODEL), layer), _const_spec((W_MIX, D_MODEL), layer),
            _const_spec((D_MODEL, D_MODEL), layer),
        ],
        out_specs=row(D_MODEL),
        out_shape=jax.ShapeDtypeStruct(h.shape, F32),
        compiler_params=pltpu.CompilerParams(
            dimension_semantics=("parallel", "parallel"), vmem_limit_bytes=VMEM_LIMIT),
        name="merge",
    )(h, gain, o_fox, o_rwkv, w_g, g_bias, w_bf, w_br, w_o)


FF_CHUNK = D_FF


def _ffn_kernel(h_ref, g_ref, wup_ref, wc_ref, wd_ref, out_ref, carry_ref, *, tm):
    @pl.when(pl.program_id(1) == 0)
    def _():
        carry_ref[...] = jnp.zeros_like(carry_ref)

    x = h_ref[...]
    hn = _rms_norm_rows(x, g_ref[...]).astype(BF16)
    wc = wc_ref[...]
    row = lax.broadcasted_iota(jnp.int32, (tm, FF_CHUNK), 0)

    def conv(u, col):
        prev = carry_ref[:, col:col + FF_CHUNK]
        p1 = prev[SUBLANES - 1:SUBLANES, :]
        p2 = prev[SUBLANES - 2:SUBLANES - 1, :]
        u1 = jnp.where(row == 0, p1, pltpu.roll(u, 1, 0))
        u2 = jnp.where(row == 0, p2, jnp.where(row == 1, p1, pltpu.roll(u, 2, 0)))
        carry_ref[:, col:col + FF_CHUNK] = u[tm - SUBLANES:tm, :]
        w = wc[:, col:col + FF_CHUNK]
        return w[0:1] * u2 + w[1:2] * u1 + w[2:3] * u

    acc = x
    for c in range(D_FF // FF_CHUNK):
        cg = c * FF_CHUNK
        cv = D_FF + c * FF_CHUNK
        gate = conv(_dot(hn, wup_ref[:, cg:cg + FF_CHUNK]), cg)
        val = conv(_dot(hn, wup_ref[:, cv:cv + FF_CHUNK]), cv)
        act = (gate * jax.nn.sigmoid(gate) * val).astype(BF16)
        acc = acc + _dot(act, wd_ref[cg:cg + FF_CHUNK, :])
    out_ref[...] = acc


def _ffn_call(h, gain, w_up, w_conv, w_down, *, layer, tm):
    b, lp, _ = h.shape
    row = pl.BlockSpec((None, tm, D_MODEL), lambda bi, i: (bi, i, 0))
    return pl.pallas_call(
        functools.partial(_ffn_kernel, tm=tm),
        grid=(b, lp // tm),
        in_specs=[
            row, _const_spec((1, D_MODEL), layer),
            _const_spec((D_MODEL, 2 * D_FF), layer), _const_spec((CONV_W, 2 * D_FF), layer),
            _const_spec((D_FF, D_MODEL), layer),
        ],
        out_specs=row,
        out_shape=jax.ShapeDtypeStruct(h.shape, F32),
        scratch_shapes=[pltpu.VMEM((SUBLANES, 2 * D_FF), F32)],
        compiler_params=pltpu.CompilerParams(
            dimension_semantics=("parallel", "arbitrary"), vmem_limit_bytes=VMEM_LIMIT),
        name="conv_ffn",
    )(h, gain, w_up, w_conv, w_down)


def _row_tile(lp):
    for t in (640, 512, 384, 256, 128):
        if lp % t == 0:
            return t
    raise ValueError(f"padded length {lp} has no supported row tile")


def _pad_cols(w, n):
    return jnp.pad(w, [(0, 0)] * (w.ndim - 1) + [(0, n - w.shape[-1])])


def _prep_in_weights(w_in, f_bias, mu):
    o_f = 3 * W_MIX
    o_rw = o_f + N_HEADS
    n_rw = 3 * W_MIX + LORA_W + LORA_A + LORA_G
    o_g = o_rw + n_rw
    w_bf = w_in.astype(BF16)

    def rw_layout(t):
        return jnp.concatenate([
            t[..., 0:3 * W_MIX],
            _pad_cols(t[..., 3 * W_MIX:3 * W_MIX + LORA_W], LANES),
            _pad_cols(t[..., 3 * W_MIX + LORA_W:3 * W_MIX + LORA_W + LORA_A], LANES),
            t[..., 3 * W_MIX + LORA_W + LORA_A:],
        ], axis=-1)

    w_a = jnp.concatenate([
        w_bf[..., 0:2 * W_MIX],
        _pad_cols(w_bf[..., o_f:o_rw], LANES),
        rw_layout(w_bf[..., o_rw:o_g]),
    ], axis=-1)
    w_vt = jnp.swapaxes(w_bf[..., 2 * W_MIX:3 * W_MIX], -1, -2)
    w_g = w_bf[..., o_g:]
    fb = _pad_cols(f_bias, LANES)[:, None, :]
    return w_a, w_vt, w_g, fb, rw_layout(mu)[:, None, :]


def kernel(x, meta_tokens, norm_mix, norm_ffn, w_in, fox_q_norm, fox_k_norm, fox_f_bias,
           rwkv_shift_mu, rwkv_w0, rwkv_w_up, rwkv_a0, rwkv_a_up, rwkv_g_up, rwkv_k_k,
           rwkv_k_a, rwkv_r_k, rwkv_gn_w, rwkv_gn_b, w_branch_fox, w_branch_rwkv,
           gate_bias, w_out, ffn_up, ffn_conv, ffn_down):
    b, s, _ = x.shape
    depth = w_in.shape[0]
    length = N_META + s
    lp = -(-length // LANES) * LANES
    tm = _row_tile(lp)

    meta = jnp.broadcast_to(meta_tokens.astype(x.dtype)[None], (b, N_META, D_MODEL))
    h = jnp.concatenate([meta, x], axis=1)
    h = jnp.pad(h, ((0, 0), (0, lp - length), (0, 0)))

    w_a, w_vt, w_g, f_b, mu = _prep_in_weights(w_in, fox_f_bias, rwkv_shift_mu)
    qk_gain = jnp.stack([jnp.tile(fox_q_norm, (1, N_HEADS)), jnp.tile(fox_k_norm, (1, N_HEADS))], axis=1)
    bd = (jnp.kron(jnp.eye(N_HEADS, dtype=F32), jnp.ones((HEAD_DIM, HEAD_DIM), F32)) / HEAD_DIM).astype(BF16)
    pad_rows = lambda w: jnp.pad(w, ((0, 0), (0, LANES - w.shape[1]), (0, 0))).astype(BF16)
    w_up_l, a_up_l, g_up_l = pad_rows(rwkv_w_up), pad_rows(rwkv_a_up), rwkv_g_up.astype(BF16)
    prm = jnp.stack([rwkv_w0, rwkv_a0, rwkv_k_k, rwkv_k_a, rwkv_r_k.reshape(depth, W_MIX),
                     rwkv_gn_w, rwkv_gn_b, jnp.zeros_like(rwkv_w0)], axis=1)
    w_bf, w_br, w_o = w_branch_fox.astype(BF16), w_branch_rwkv.astype(BF16), w_out.astype(BF16)
    f_up, f_down = ffn_up.astype(BF16), ffn_down.astype(BF16)
    g_mix, g_ffn, g_bias = norm_mix[:, None, :], norm_ffn[:, None, :], gate_bias[:, None, :]

    for i in range(depth):
        q, kx, vt, rw = _in_call(h, g_mix, w_a, w_vt, qk_gain, bd, f_b, layer=i, tm=tm)
        o_fox = _fox_call(q, kx, vt, tq=tm)
        o_rwkv = _rwkv_call(rw, mu, w_up_l, a_up_l, g_up_l, prm, layer=i)
        h = _merge_call(h, g_mix, o_fox, o_rwkv, w_g, g_bias, w_bf, w_br, w_o, layer=i, tm=tm)
        h = _ffn_call(h, g_ffn, f_up, ffn_conv, f_down, layer=i, tm=tm)
    return h[:, N_META:N_META + s]
```
